```python
import math
import jax
import jax.numpy as jnp
from jax import lax
import numpy as np

D_MODEL = 1024
BATCH = 2
SEQ = 8192
DEPTH = 2
DEC_BATCH = 32
DEC_SEQ = 8
PAST_LEN = 16384
PAGE_SIZE = 128

HA = 4
NA = 64
WA = HA * NA
LORA_W = 64
LORA_A = 64
LORA_G = 128
NCA = 3 * WA + LORA_W + LORA_A + LORA_G
RWKV_GN_EPS = 64e-5
HB = 4
DHB = 64
DVB = 2 * DHB
WB = HB * DVB
NCB = 2 * (HB * 2 * DHB) + WB
DIFF_SUBLN_EPS = 1e-5
HC = 4
DHC = 64
WC = HC * DHC
HI = 8
DI = 64
TOPK_MAX = 256
NCC = 3 * WC + HI * DI + DI + HI
N_BRANCH = 3
N_IN = NCA + NCB + NCC + N_BRANCH * D_MODEL
D_FF = -(-8 * D_MODEL // (3 * 256)) * 256
QBLOCK = 128
NORM_EPS = 1e-6

kernel_name = 'hybrid_rwkv7_diffattn_dsa_step'


def split_last(z, sizes):
    offs = []
    acc = 0
    for s in sizes[:-1]:
        acc += s
        offs.append(acc)
    return jnp.split(z, offs, axis=-1)


def rmsnorm(x, g, eps=NORM_EPS):
    xf = x.astype(jnp.float32)
    y = xf * lax.rsqrt(jnp.mean(xf * xf, axis=-1, keepdims=True) + eps)
    return (y * g.astype(jnp.float32)).astype(x.dtype)


def to_blocks(z):
    b, s = z.shape[:2]
    return z.reshape(b, s // QBLOCK, QBLOCK, *z.shape[2:]).swapaxes(0, 1)


def from_blocks(z):
    nb, b = z.shape[:2]
    return z.swapaxes(0, 1).reshape(b, nb * QBLOCK, *z.shape[3:])


def rwkv7_mix(pa, prev, s0, mu, w0, w_up, a0, a_up, g_up, k_k, k_a, r_k, ln_w, ln_b):
    B, T, _ = pa.shape
    f = lambda z: z.astype(jnp.float32)
    p = f(pa)
    shifted = jnp.concatenate([f(prev)[:, None], p[:, :-1]], axis=1)
    px = p + (shifted - p) * f(mu)
    r, k, v, wd, ad, gd = split_last(px, [WA, WA, WA, LORA_W, LORA_A, LORA_G])
    w = -jax.nn.softplus(-(f(w0) + jnp.tanh(wd) @ f(w_up))) - 0.5
    decay = jnp.exp(-jnp.exp(w))
    a = jax.nn.sigmoid(f(a0) + ad @ f(a_up))
    g = jax.nn.sigmoid(gd) @ f(g_up)
    hd = lambda z: z.reshape(B, T, HA, NA)
    kk = hd(k * f(k_k))
    kk = kk * lax.rsqrt(jnp.maximum(jnp.sum(kk * kk, -1, keepdims=True), 1e-24))
    k = k * (1.0 + (a - 1.0) * f(k_a))
    r, k, v, a, decay = hd(r), hd(k), hd(v), hd(a), hd(decay)

    def step(S, inp):
        r_t, w_t, k_t, v_t, kk_t, a_t = inp
        sk = jnp.einsum('bhij,bhj->bhi', S, kk_t)
        S = S * w_t[:, :, None, :] - sk[..., None] * (kk_t * a_t)[:, :, None, :] + v_t[..., None] * k_t[:, :, None, :]
        return S, jnp.einsum('bhij,bhj->bhi', S, r_t)

    xs = tuple(jnp.moveaxis(z, 1, 0) for z in (r, decay, k, v, kk, a))
    s_fin, o = lax.scan(step, f(s0), xs)
    o = jnp.moveaxis(o, 0, 1)
    mean = jnp.mean(o, -1, keepdims=True)
    var = jnp.mean(jnp.square(o - mean), -1, keepdims=True)
    o = ((o - mean) * lax.rsqrt(var + RWKV_GN_EPS)).reshape(B, T, WA) * f(ln_w) + f(ln_b)
    bonus = jnp.sum(r * k * f(r_k), -1, keepdims=True) * v
    o = (o + bonus.reshape(B, T, WA)) * g
    return o, s_fin.astype(s0.dtype), pa[:, -1]


def diff_attend(q, k, v, mask, lam):
    s = jnp.einsum('bqhcd,bkhcd->bhcqk', q.astype(jnp.float32), k.astype(jnp.float32)) * (DHB ** -0.5)
    p = jax.nn.softmax(jnp.where(mask, s, -jnp.inf), axis=-1)
    pd = p[:, :, 0] - lam * p[:, :, 1]
    return jnp.einsum('bhqk,bkhd->bqhd', pd, v.astype(jnp.float32))


def diff_prompt(q, k, v, lam):
    S = q.shape[1]
    kpos = jnp.arange(S)

    def block(args):
        q_i, i = args
        qpos = i * QBLOCK + jnp.arange(QBLOCK)
        return diff_attend(q_i, k, v, qpos[:, None] >= kpos[None, :], lam)

    return from_blocks(lax.map(block, (to_blocks(q), jnp.arange(S // QBLOCK))))


def diff_sample(q, k_new, v_new, pool_k, pool_v, page_table, l, lam):
    T = q.shape[1]
    past = page_table.shape[1] * PAGE_SIZE
    mask = jnp.concatenate([jnp.ones((T, past), bool), jnp.tril(jnp.ones((T, T), bool))], axis=1)

    def one(args):
        q_b, kn_b, vn_b, pt_b = args
        kp = pool_k[l, pt_b].reshape(past, HB, 2, DHB)
        vp = pool_v[l, pt_b].reshape(past, HB, DVB)
        kc = jnp.concatenate([kp, kn_b], axis=0)[None]
        vc = jnp.concatenate([vp, vn_b], axis=0)[None]
        return diff_attend(q_b[None], kc, vc, mask, lam)[0]

    return lax.map(one, (q, k_new, v_new, page_table))


def indexer_scores(qi, wi, ki):
    qk = jnp.einsum('bthd,bsd->bths', qi.astype(jnp.float32), ki.astype(jnp.float32))
    return jnp.einsum('bth,bths->bts', wi.astype(jnp.float32), jax.nn.relu(qk))


def sparse_attend(q, k_sel, v_sel, valid):
    s = jnp.einsum('bthd,btkhd->bthk', q.astype(jnp.float32), k_sel.astype(jnp.float32)) * (DHC ** -0.5)
    p = jax.nn.softmax(jnp.where(valid[:, :, None, :], s, -jnp.inf), axis=-1)
    return jnp.einsum('bthk,btkhd->bthd', p, v_sel.astype(jnp.float32))


def dsa_prompt(q, k, v, qi, wi, ki):
    B, S = q.shape[:2]
    topk = min(TOPK_MAX, S // 4)
    kpos = jnp.arange(S)
    b_idx = jnp.arange(B)[:, None, None]

    def block(args):
        q_i, qi_i, wi_i, i = args
        qpos = i * QBLOCK + jnp.arange(QBLOCK)
        sc = jnp.where(qpos[:, None] >= kpos[None, :], indexer_scores(qi_i, wi_i, ki), -jnp.inf)
        _, sel = lax.top_k(sc, topk)
        return sparse_attend(q_i, k[b_idx, sel], v[b_idx, sel], sel <= qpos[None, :, None])

    o = lax.map(block, (to_blocks(q), to_blocks(qi), to_blocks(wi), jnp.arange(S // QBLOCK)))
    return from_blocks(o)


def dsa_sample(q, k_new, v_new, qi, wi, ki_new, pool_k, pool_v, pool_ki, page_table, l):
    DB, T = q.shape[:2]
    past = page_table.shape[1] * PAGE_SIZE
    L = past + T
    topk = min(TOPK_MAX, L // 4)
    ki_all = jnp.concatenate([pool_ki[l, page_table].reshape(DB, past, DI), ki_new], axis=1)
    qpos = past + jnp.arange(T)
    sc = jnp.where(qpos[:, None] >= jnp.arange(L)[None, :], indexer_scores(qi, wi, ki_all), -jnp.inf)
    _, sel = lax.top_k(sc, topk)
    b_idx = jnp.arange(DB)[:, None, None]
    from_past = (sel < past)[..., None, None]
    s_p = jnp.minimum(sel, past - 1)
    phys = page_table[b_idx, s_p // PAGE_SIZE]
    off = s_p % PAGE_SIZE
    s_n = jnp.clip(sel - past, 0, T - 1)
    k_sel = jnp.where(from_past, pool_k[l, phys, off], k_new[b_idx, s_n])
    v_sel = jnp.where(from_past, pool_v[l, phys, off], v_new[b_idx, s_n])
    return sparse_attend(q, k_sel, v_sel, sel <= qpos[None, :, None])


def setup_inputs(seed: int = 0) -> dict:
    key = jax.random.key(seed)
    keys = jax.random.split(key, 48)
    counter = [0]

    def nk():
        counter[0] += 1
        return keys[counter[0] - 1]

    def nrm(shape, scale):
        return jax.random.normal(nk(), shape, jnp.float32) * scale

    n_pages = PAST_LEN // PAGE_SIZE
    n_pool = (5 * DEC_BATCH * n_pages + 3) // 4
    L = DEPTH
    D = D_MODEL
    inp = {}
    inp['x_prompt'] = nrm((BATCH, SEQ, D), 1.0)
    inp['x_sample'] = nrm((DEC_BATCH, DEC_SEQ, D), 1.0)
    inp['state_rwkv'] = nrm((L, DEC_BATCH, HA, NA, NA), 0.3)
    inp['state_shift'] = nrm((L, DEC_BATCH, NCA), 1.0)
    inp['cache_diff_k'] = nrm((L, n_pool, PAGE_SIZE, HB, 2, DHB), 1.0)
    inp['cache_diff_v'] = nrm((L, n_pool, PAGE_SIZE, HB, DVB), 1.0)
    inp['cache_dsa_k'] = nrm((L, n_pool, PAGE_SIZE, HC, DHC), 1.0)
    inp['cache_dsa_v'] = nrm((L, n_pool, PAGE_SIZE, HC, DHC), 1.0)
    inp['cache_idx_k'] = nrm((L, n_pool, PAGE_SIZE, DI), 1.0)
    inp['page_table'] = jax.random.permutation(nk(), n_pool)[:DEC_BATCH * n_pages].reshape(DEC_BATCH, n_pages).astype(jnp.int32)
    inp['g_mix'] = 1.0 + nrm((L, D), 0.01)
    inp['w_in'] = nrm((L, D, N_IN), D ** -0.5)
    inp['rwkv_mu'] = jax.random.uniform(nk(), (L, NCA), jnp.float32)
    inp['rwkv_w0'] = nrm((L, WA), 0.5)
    inp['rwkv_w_up'] = nrm((L, LORA_W, WA), 0.1)
    inp['rwkv_a0'] = nrm((L, WA), 0.1)
    inp['rwkv_a_up'] = nrm((L, LORA_A, WA), LORA_A ** -0.5)
    inp['rwkv_g_up'] = nrm((L, LORA_G, WA), LORA_G ** -0.5)
    inp['rwkv_k_k'] = 0.85 + nrm((L, WA), 0.02)
    inp['rwkv_k_a'] = 1.0 + nrm((L, WA), 0.02)
    inp['rwkv_r_k'] = nrm((L, HA, NA), 0.1)
    inp['rwkv_ln_w'] = 1.0 + nrm((L, WA), 0.01)
    inp['rwkv_ln_b'] = nrm((L, WA), 0.01)
    inp['diff_lam_q1'] = nrm((L, DHB), 0.1)
    inp['diff_lam_k1'] = nrm((L, DHB), 0.1)
    inp['diff_lam_q2'] = nrm((L, DHB), 0.1)
    inp['diff_lam_k2'] = nrm((L, DHB), 0.1)
    inp['diff_subln_g'] = 1.0 + nrm((L, DVB), 0.01)
    inp['w_br_a'] = nrm((L, WA, D), WA ** -0.5)
    inp['w_br_b'] = nrm((L, WB, D), WB ** -0.5)
    inp['w_br_c'] = nrm((L, WC, D), WC ** -0.5)
    inp['w_out'] = nrm((L, D, D), D ** -0.5)
    inp['g_ffn'] = 1.0 + nrm((L, D), 0.01)
    inp['w_ffn_gate'] = nrm((L, D, D_FF), D ** -0.5)
    inp['w_ffn_up'] = nrm((L, D, D_FF), D ** -0.5)
    inp['w_ffn_down'] = nrm((L, D_FF, D), D_FF ** -0.5)
    inp['g_final'] = 1.0 + nrm((D,), 0.01)
    return inp


def reference(x_prompt, x_sample, state_rwkv, state_shift, cache_diff_k, cache_diff_v, cache_dsa_k,
              cache_dsa_v, cache_idx_k, page_table, g_mix, w_in, rwkv_mu, rwkv_w0, rwkv_w_up, rwkv_a0,
              rwkv_a_up, rwkv_g_up, rwkv_k_k, rwkv_k_a, rwkv_r_k, rwkv_ln_w, rwkv_ln_b, diff_lam_q1,
              diff_lam_k1, diff_lam_q2, diff_lam_k2, diff_subln_g, w_br_a, w_br_b, w_br_c, w_out, g_ffn,
              w_ffn_gate, w_ffn_up, w_ffn_down, g_final):
    B = x_prompt.shape[0]

    def project(h, l):
        b, t = h.shape[:2]
        xn = rmsnorm(h, g_mix[l])
        pa, pb, pc, pg = split_last(xn @ w_in[l], [NCA, NCB, NCC, N_BRANCH * D_MODEL])
        qb, kb, vb = split_last(pb, [HB * 2 * DHB, HB * 2 * DHB, WB])
        qc, kc, vc, qi, ki, wi = split_last(pc, [WC, WC, WC, HI * DI, DI, HI])
        diff = (qb.reshape(b, t, HB, 2, DHB), kb.reshape(b, t, HB, 2, DHB), vb.reshape(b, t, HB, DVB))
        dsa = (qc.reshape(b, t, HC, DHC), kc.reshape(b, t, HC, DHC), vc.reshape(b, t, HC, DHC),
               qi.reshape(b, t, HI, DI) * (DI ** -0.5), wi * (HI ** -0.5), ki)
        gates = jax.nn.sigmoid(pg).reshape(b, t, N_BRANCH, D_MODEL)
        return pa, diff, dsa, gates

    def finish(h, oa, ob, oc, gates, l, lam_init):
        b, t = h.shape[:2]
        ob = rmsnorm(ob, diff_subln_g[l], DIFF_SUBLN_EPS) * (1.0 - lam_init)
        br_a = oa.astype(h.dtype) @ w_br_a[l]
        br_b = ob.reshape(b, t, WB).astype(h.dtype) @ w_br_b[l]
        br_c = oc.reshape(b, t, WC).astype(h.dtype) @ w_br_c[l]
        merged = gates[:, :, 0] * br_a + gates[:, :, 1] * br_b + gates[:, :, 2] * br_c
        h = h + merged @ w_out[l]
        xn = rmsnorm(h, g_ffn[l])
        return h + (jax.nn.silu(xn @ w_ffn_gate[l]) * (xn @ w_ffn_up[l])) @ w_ffn_down[l]

    names = ('rwkv', 'shift', 'dk', 'dv', 'ck', 'cv', 'ik')
    newp = {n: [] for n in names}
    news = {n: [] for n in names}
    hp, hs = x_prompt, x_sample
    for l in range(DEPTH):
        lam_init = 0.8 - 0.6 * math.exp(-0.3 * l)
        lam = (jnp.exp(jnp.sum(diff_lam_q1[l] * diff_lam_k1[l])) - jnp.exp(jnp.sum(diff_lam_q2[l] * diff_lam_k2[l]))
               + lam_init).astype(jnp.float32)
        rw = (rwkv_mu[l], rwkv_w0[l], rwkv_w_up[l], rwkv_a0[l], rwkv_a_up[l], rwkv_g_up[l], rwkv_k_k[l],
              rwkv_k_a[l], rwkv_r_k[l], rwkv_ln_w[l], rwkv_ln_b[l])

        pa, (qb, kb, vb), (qc, kc, vc, qi, wi, ki), gates = project(hp, l)
        oa, s_a, sh_a = rwkv7_mix(pa, jnp.zeros((B, NCA), pa.dtype),
                                  jnp.zeros((B, HA, NA, NA), state_rwkv.dtype), *rw)
        ob = diff_prompt(qb, kb, vb, lam)
        oc = dsa_prompt(qc, kc, vc, qi, wi, ki)
        hp = finish(hp, oa, ob, oc, gates, l, lam_init)
        for n, val in zip(names, (s_a, sh_a, kb, vb, kc, vc, ki)):
            newp[n].append(val)

        pa, (qb, kb, vb), (qc, kc, vc, qi, wi, ki), gates = project(hs, l)
        oa, s_a, sh_a = rwkv7_mix(pa, state_shift[l], state_rwkv[l], *rw)
        ob = diff_sample(qb, kb, vb, cache_diff_k, cache_diff_v, page_table, l, lam)
        oc = dsa_sample(qc, kc, vc, qi, wi, ki, cache_dsa_k, cache_dsa_v, cache_idx_k, page_table, l)
        hs = finish(hs, oa, ob, oc, gates, l, lam_init)
        for n, val in zip(names, (s_a, sh_a, kb, vb, kc, vc, ki)):
            news[n].append(val)

    return (rmsnorm(hp, g_final), rmsnorm(hs, g_final),
            jnp.stack(newp['rwkv']), jnp.stack(newp['shift']), jnp.stack(newp['dk']), jnp.stack(newp['dv']),
            jnp.stack(newp['ck']), jnp.stack(newp['cv']), jnp.stack(newp['ik']),
            jnp.stack(news['rwkv']), jnp.stack(news['shift']), jnp.stack(news['dk']), jnp.stack(news['dv']),
            jnp.stack(news['ck']), jnp.stack(news['cv']), jnp.stack(news['ik']))
```

```python
import functools
import math
import jax
import jax.numpy as jnp
from jax import lax
import numpy as np
from jax.experimental import pallas as pl
from jax.experimental.pallas import tpu as pltpu

D_MODEL = 1024
BATCH = 2
SEQ = 8192
DEPTH = 2
DEC_BATCH = 32
DEC_SEQ = 8
PAST_LEN = 16384
PAGE_SIZE = 128

HA = 4
NA = 64
WA = HA * NA
LORA_W = 64
LORA_A = 64
LORA_G = 128
NCA = 3 * WA + LORA_W + LORA_A + LORA_G
RWKV_GN_EPS = 64e-5
HB = 4
DHB = 64
DVB = 2 * DHB
WB = HB * DVB
NCB = 2 * (HB * 2 * DHB) + WB
DIFF_SUBLN_EPS = 1e-5
HC = 4
DHC = 64
WC = HC * DHC
HI = 8
DI = 64
TOPK_MAX = 256
NCC = 3 * WC + HI * DI + DI + HI
N_BRANCH = 3
N_IN = NCA + NCB + NCC + N_BRANCH * D_MODEL
D_FF = -(-8 * D_MODEL // (3 * 256)) * 256
QBLOCK = 128
NORM_EPS = 1e-6


def _rmsnorm_body(x_ref, g_ref, o_ref, *, eps):
    x = x_ref[...]
    y = x * lax.rsqrt(jnp.mean(x * x, axis=-1, keepdims=True) + eps)
    o_ref[...] = y * g_ref[...]


def rmsnorm_pallas(x2d, g, eps=NORM_EPS, tm=256):
    t, d = x2d.shape
    return pl.pallas_call(
        functools.partial(_rmsnorm_body, eps=eps),
        grid=(t // tm,),
        in_specs=[pl.BlockSpec((tm, d), lambda i: (i, 0)), pl.BlockSpec((1, d), lambda i: (0, 0))],
        out_specs=pl.BlockSpec((tm, d), lambda i: (i, 0)),
        out_shape=jax.ShapeDtypeStruct((t, d), jnp.float32),
        name="final_rmsnorm",
    )(x2d, g.reshape(1, d))


def split_last(z, sizes):
    offs = []
    acc = 0
    for s in sizes[:-1]:
        acc += s
        offs.append(acc)
    return jnp.split(z, offs, axis=-1)


def rmsnorm(x, g, eps=NORM_EPS):
    xf = x.astype(jnp.float32)
    y = xf * lax.rsqrt(jnp.mean(xf * xf, axis=-1, keepdims=True) + eps)
    return (y * g.astype(jnp.float32)).astype(x.dtype)


def to_blocks(z):
    b, s = z.shape[:2]
    return z.reshape(b, s // QBLOCK, QBLOCK, *z.shape[2:]).swapaxes(0, 1)


def from_blocks(z):
    nb, b = z.shape[:2]
    return z.swapaxes(0, 1).reshape(b, nb * QBLOCK, *z.shape[3:])


def rwkv7_mix(pa, prev, s0, mu, w0, w_up, a0, a_up, g_up, k_k, k_a, r_k, ln_w, ln_b):
    B, T, _ = pa.shape
    f = lambda z: z.astype(jnp.float32)
    p = f(pa)
    shifted = jnp.concatenate([f(prev)[:, None], p[:, :-1]], axis=1)
    px = p + (shifted - p) * f(mu)
    r, k, v, wd, ad, gd = split_last(px, [WA, WA, WA, LORA_W, LORA_A, LORA_G])
    w = -jax.nn.softplus(-(f(w0) + jnp.tanh(wd) @ f(w_up))) - 0.5
    decay = jnp.exp(-jnp.exp(w))
    a = jax.nn.sigmoid(f(a0) + ad @ f(a_up))
    g = jax.nn.sigmoid(gd) @ f(g_up)
    hd = lambda z: z.reshape(B, T, HA, NA)
    kk = hd(k * f(k_k))
    kk = kk * lax.rsqrt(jnp.maximum(jnp.sum(kk * kk, -1, keepdims=True), 1e-24))
    k = k * (1.0 + (a - 1.0) * f(k_a))
    r, k, v, a, decay = hd(r), hd(k), hd(v), hd(a), hd(decay)

    def step(S, inp):
        r_t, w_t, k_t, v_t, kk_t, a_t = inp
        sk = jnp.einsum('bhij,bhj->bhi', S, kk_t)
        S = S * w_t[:, :, None, :] - sk[..., None] * (kk_t * a_t)[:, :, None, :] + v_t[..., None] * k_t[:, :, None, :]
        return S, jnp.einsum('bhij,bhj->bhi', S, r_t)

    xs = tuple(jnp.moveaxis(z, 1, 0) for z in (r, decay, k, v, kk, a))
    s_fin, o = lax.scan(step, f(s0), xs)
    o = jnp.moveaxis(o, 0, 1)
    mean = jnp.mean(o, -1, keepdims=True)
    var = jnp.mean(jnp.square(o - mean), -1, keepdims=True)
    o = ((o - mean) * lax.rsqrt(var + RWKV_GN_EPS)).reshape(B, T, WA) * f(ln_w) + f(ln_b)
    bonus = jnp.sum(r * k * f(r_k), -1, keepdims=True) * v
    o = (o + bonus.reshape(B, T, WA)) * g
    return o, s_fin.astype(s0.dtype), pa[:, -1]


def diff_attend(q, k, v, mask, lam):
    s = jnp.einsum('bqhcd,bkhcd->bhcqk', q.astype(jnp.float32), k.astype(jnp.float32)) * (DHB ** -0.5)
    p = jax.nn.softmax(jnp.where(mask, s, -jnp.inf), axis=-1)
    pd = p[:, :, 0] - lam * p[:, :, 1]
    return jnp.einsum('bhqk,bkhd->bqhd', pd, v.astype(jnp.float32))


def diff_prompt(q, k, v, lam):
    S = q.shape[1]
    kpos = jnp.arange(S)

    def block(args):
        q_i, i = args
        qpos = i * QBLOCK + jnp.arange(QBLOCK)
        return diff_attend(q_i, k, v, qpos[:, None] >= kpos[None, :], lam)

    return from_blocks(lax.map(block, (to_blocks(q), jnp.arange(S // QBLOCK))))


def diff_sample(q, k_new, v_new, pool_k, pool_v, page_table, l, lam):
    T = q.shape[1]
    past = page_table.shape[1] * PAGE_SIZE
    mask = jnp.concatenate([jnp.ones((T, past), bool), jnp.tril(jnp.ones((T, T), bool))], axis=1)

    def one(args):
        q_b, kn_b, vn_b, pt_b = args
        kp = pool_k[l, pt_b].reshape(past, HB, 2, DHB)
        vp = pool_v[l, pt_b].reshape(past, HB, DVB)
        kc = jnp.concatenate([kp, kn_b], axis=0)[None]
        vc = jnp.concatenate([vp, vn_b], axis=0)[None]
        return diff_attend(q_b[None], kc, vc, mask, lam)[0]

    return lax.map(one, (q, k_new, v_new, page_table))


def indexer_scores(qi, wi, ki):
    qk = jnp.einsum('bthd,bsd->bths', qi.astype(jnp.float32), ki.astype(jnp.float32))
    return jnp.einsum('bth,bths->bts', wi.astype(jnp.float32), jax.nn.relu(qk))


def sparse_attend(q, k_sel, v_sel, valid):
    s = jnp.einsum('bthd,btkhd->bthk', q.astype(jnp.float32), k_sel.astype(jnp.float32)) * (DHC ** -0.5)
    p = jax.nn.softmax(jnp.where(valid[:, :, None, :], s, -jnp.inf), axis=-1)
    return jnp.einsum('bthk,btkhd->bthd', p, v_sel.astype(jnp.float32))


def dsa_prompt(q, k, v, qi, wi, ki):
    B, S = q.shape[:2]
    topk = min(TOPK_MAX, S // 4)
    kpos = jnp.arange(S)
    b_idx = jnp.arange(B)[:, None, None]

    def block(args):
        q_i, qi_i, wi_i, i = args
        qpos = i * QBLOCK + jnp.arange(QBLOCK)
        sc = jnp.where(qpos[:, None] >= kpos[None, :], indexer_scores(qi_i, wi_i, ki), -jnp.inf)
        _, sel = lax.top_k(sc, topk)
        return sparse_attend(q_i, k[b_idx, sel], v[b_idx, sel], sel <= qpos[None, :, None])

    o = lax.map(block, (to_blocks(q), to_blocks(qi), to_blocks(wi), jnp.arange(S // QBLOCK)))
    return from_blocks(o)


def dsa_sample(q, k_new, v_new, qi, wi, ki_new, pool_k, pool_v, pool_ki, page_table, l):
    DB, T = q.shape[:2]
    past = page_table.shape[1] * PAGE_SIZE
    L = past + T
    topk = min(TOPK_MAX, L // 4)
    ki_all = jnp.concatenate([pool_ki[l, page_table].reshape(DB, past, DI), ki_new], axis=1)
    qpos = past + jnp.arange(T)
    sc = jnp.where(qpos[:, None] >= jnp.arange(L)[None, :], indexer_scores(qi, wi, ki_all), -jnp.inf)
    _, sel = lax.top_k(sc, topk)
    b_idx = jnp.arange(DB)[:, None, None]
    from_past = (sel < past)[..., None, None]
    s_p = jnp.minimum(sel, past - 1)
    phys = page_table[b_idx, s_p // PAGE_SIZE]
    off = s_p % PAGE_SIZE
    s_n = jnp.clip(sel - past, 0, T - 1)
    k_sel = jnp.where(from_past, pool_k[l, phys, off], k_new[b_idx, s_n])
    v_sel = jnp.where(from_past, pool_v[l, phys, off], v_new[b_idx, s_n])
    return sparse_attend(q, k_sel, v_sel, sel <= qpos[None, :, None])


def kernel(x_prompt, x_sample, state_rwkv, state_shift, cache_diff_k, cache_diff_v, cache_dsa_k,
           cache_dsa_v, cache_idx_k, page_table, g_mix, w_in, rwkv_mu, rwkv_w0, rwkv_w_up, rwkv_a0,
           rwkv_a_up, rwkv_g_up, rwkv_k_k, rwkv_k_a, rwkv_r_k, rwkv_ln_w, rwkv_ln_b, diff_lam_q1,
           diff_lam_k1, diff_lam_q2, diff_lam_k2, diff_subln_g, w_br_a, w_br_b, w_br_c, w_out, g_ffn,
           w_ffn_gate, w_ffn_up, w_ffn_down, g_final):
    B = x_prompt.shape[0]

    def project(h, l):
        b, t = h.shape[:2]
        xn = rmsnorm(h, g_mix[l])
        pa, pb, pc, pg = split_last(xn @ w_in[l], [NCA, NCB, NCC, N_BRANCH * D_MODEL])
        qb, kb, vb = split_last(pb, [HB * 2 * DHB, HB * 2 * DHB, WB])
        qc, kc, vc, qi, ki, wi = split_last(pc, [WC, WC, WC, HI * DI, DI, HI])
        diff = (qb.reshape(b, t, HB, 2, DHB), kb.reshape(b, t, HB, 2, DHB), vb.reshape(b, t, HB, DVB))
        dsa = (qc.reshape(b, t, HC, DHC), kc.reshape(b, t, HC, DHC), vc.reshape(b, t, HC, DHC),
               qi.reshape(b, t, HI, DI) * (DI ** -0.5), wi * (HI ** -0.5), ki)
        gates = jax.nn.sigmoid(pg).reshape(b, t, N_BRANCH, D_MODEL)
        return pa, diff, dsa, gates

    def finish(h, oa, ob, oc, gates, l, lam_init):
        b, t = h.shape[:2]
        ob = rmsnorm(ob, diff_subln_g[l], DIFF_SUBLN_EPS) * (1.0 - lam_init)
        br_a = oa.astype(h.dtype) @ w_br_a[l]
        br_b = ob.reshape(b, t, WB).astype(h.dtype) @ w_br_b[l]
        br_c = oc.reshape(b, t, WC).astype(h.dtype) @ w_br_c[l]
        merged = gates[:, :, 0] * br_a + gates[:, :, 1] * br_b + gates[:, :, 2] * br_c
        h = h + merged @ w_out[l]
        xn = rmsnorm(h, g_ffn[l])
        return h + (jax.nn.silu(xn @ w_ffn_gate[l]) * (xn @ w_ffn_up[l])) @ w_ffn_down[l]

    names = ('rwkv', 'shift', 'dk', 'dv', 'ck', 'cv', 'ik')
    newp = {n: [] for n in names}
    news = {n: [] for n in names}
    hp, hs = x_prompt, x_sample
    for l in range(DEPTH):
        lam_init = 0.8 - 0.6 * math.exp(-0.3 * l)
        lam = (jnp.exp(jnp.sum(diff_lam_q1[l] * diff_lam_k1[l])) - jnp.exp(jnp.sum(diff_lam_q2[l] * diff_lam_k2[l]))
               + lam_init).astype(jnp.float32)
        rw = (rwkv_mu[l], rwkv_w0[l], rwkv_w_up[l], rwkv_a0[l], rwkv_a_up[l], rwkv_g_up[l], rwkv_k_k[l],
              rwkv_k_a[l], rwkv_r_k[l], rwkv_ln_w[l], rwkv_ln_b[l])

        pa, (qb, kb, vb), (qc, kc, vc, qi, wi, ki), gates = project(hp, l)
        oa, s_a, sh_a = rwkv7_mix(pa, jnp.zeros((B, NCA), pa.dtype),
                                  jnp.zeros((B, HA, NA, NA), state_rwkv.dtype), *rw)
        ob = diff_prompt(qb, kb, vb, lam)
        oc = dsa_prompt(qc, kc, vc, qi, wi, ki)
        hp = finish(hp, oa, ob, oc, gates, l, lam_init)
        for n, val in zip(names, (s_a, sh_a, kb, vb, kc, vc, ki)):
            newp[n].append(val)

        pa, (qb, kb, vb), (qc, kc, vc, qi, wi, ki), gates = project(hs, l)
        oa, s_a, sh_a = rwkv7_mix(pa, state_shift[l], state_rwkv[l], *rw)
        ob = diff_sample(qb, kb, vb, cache_diff_k, cache_diff_v, page_table, l, lam)
        oc = dsa_sample(qc, kc, vc, qi, wi, ki, cache_dsa_k, cache_dsa_v, cache_idx_k, page_table, l)
        hs = finish(hs, oa, ob, oc, gates, l, lam_init)
        for n, val in zip(names, (s_a, sh_a, kb, vb, kc, vc, ki)):
            news[n].append(val)

    yp = rmsnorm_pallas(hp.reshape(-1, D_MODEL), g_final).reshape(hp.shape)
    ys = rmsnorm_pallas(hs.reshape(-1, D_MODEL), g_final).reshape(hs.shape)
    return (yp, ys,
            jnp.stack(newp['rwkv']), jnp.stack(newp['shift']), jnp.stack(newp['dk']), jnp.stack(newp['dv']),
            jnp.stack(newp['ck']), jnp.stack(newp['cv']), jnp.stack(newp['ik']),
            jnp.stack(news['rwkv']), jnp.stack(news['shift']), jnp.stack(news['dk']), jnp.stack(news['dv']),
            jnp.stack(news['ck']), jnp.stack(news['cv']), jnp.stack(news['ik']))
```

```python
import functools
import math
import jax
import jax.numpy as jnp
from jax import lax
import numpy as np
from jax.experimental import pallas as pl
from jax.experimental.pallas import tpu as pltpu

D_MODEL = 1024
BATCH = 2
SEQ = 8192
DEPTH = 2
DEC_BATCH = 32
DEC_SEQ = 8
PAST_LEN = 16384
PAGE_SIZE = 128

HA = 4
NA = 64
WA = HA * NA
LORA_W = 64
LORA_A = 64
LORA_G = 128
NCA = 3 * WA + LORA_W + LORA_A + LORA_G
RWKV_GN_EPS = 64e-5
HB = 4
DHB = 64
DVB = 2 * DHB
WB = HB * DVB
NCB = 2 * (HB * 2 * DHB) + WB
DIFF_SUBLN_EPS = 1e-5
HC = 4
DHC = 64
WC = HC * DHC
HI = 8
DI = 64
TOPK_MAX = 256
NCC = 3 * WC + HI * DI + DI + HI
N_BRANCH = 3
N_IN = NCA + NCB + NCC + N_BRANCH * D_MODEL
D_FF = -(-8 * D_MODEL // (3 * 256)) * 256
QBLOCK = 128
NORM_EPS = 1e-6


def _rmsnorm_body(x_ref, g_ref, o_ref, *, eps):
    x = x_ref[...]
    y = x * lax.rsqrt(jnp.mean(x * x, axis=-1, keepdims=True) + eps)
    o_ref[...] = y * g_ref[...]


def rmsnorm_pallas(x2d, g, eps=NORM_EPS, tm=256):
    t, d = x2d.shape
    return pl.pallas_call(
        functools.partial(_rmsnorm_body, eps=eps),
        grid=(t // tm,),
        in_specs=[pl.BlockSpec((tm, d), lambda i: (i, 0)), pl.BlockSpec((1, d), lambda i: (0, 0))],
        out_specs=pl.BlockSpec((tm, d), lambda i: (i, 0)),
        out_shape=jax.ShapeDtypeStruct((t, d), jnp.float32),
        name="final_rmsnorm",
    )(x2d, g.reshape(1, d))


VMEM_LIMIT_BYTES = 56 * 1024 * 1024
LANES = 128
MASK_BIAS = -1e30
M_INIT = -0.5e30
INT32_MIN = -(2 ** 31)


def _flash_update(s, v_bf16, m_ref, l_ref, acc_ref, slot, dv):
    m_prev = m_ref[slot]
    m_new = jnp.maximum(m_prev, jnp.max(s, axis=1, keepdims=True))
    alpha = jnp.exp(m_prev - m_new)
    p = jnp.exp(s - m_new[:, :1])
    l_ref[slot] = alpha * l_ref[slot] + jnp.sum(p, axis=1, keepdims=True)
    pv = jnp.dot(p.astype(jnp.bfloat16), v_bf16, preferred_element_type=jnp.float32)
    acc_ref[slot] = acc_ref[slot] * alpha[:, :dv] + pv
    m_ref[slot] = m_new


def _dsa_prompt_body(qi_ref, wi_ref, q_ref, kit_ref, kt_ref, v_ref, o_ref,
                     key_ref, wb_ref, thr_ref, tie_ref, m_ref, l_ref, acc_ref, *, tq, tk, tr, topk, idx_bits):
    i = pl.program_id(1)
    nkc = (i + 1) * (tq // tk)
    qpos = i * tq + lax.broadcasted_iota(jnp.int32, (tq, 1), 0)

    for h in range(HI):
        wb_ref[h] = jnp.broadcast_to(wi_ref[0, :, h:h + 1], (tq, LANES))

    def score_body(c, carry):
        kit = kit_ref[0, c]
        tiles = [jnp.zeros((tq, LANES), jnp.float32)] * (tk // LANES)
        for h in range(HI):
            qk = jnp.dot(qi_ref[0, :, h * DI:(h + 1) * DI], kit, preferred_element_type=jnp.float32)
            w_b = wb_ref[h]
            tiles = [a + w_b * jnp.maximum(qk[:, j * LANES:(j + 1) * LANES], 0.0) for j, a in enumerate(tiles)]
        acc = jnp.concatenate(tiles, axis=1)
        acc = jnp.where(acc == 0.0, 0.0, acc)
        kpos = c * tk + lax.broadcasted_iota(jnp.int32, (1, tk), 1)
        acc = jnp.where(kpos <= qpos, acc, -jnp.inf)
        bits = pltpu.bitcast(acc, jnp.int32)
        key_ref[c] = bits ^ ((bits >> 31) & 0x7FFFFFFF)
        return carry

    lax.fori_loop(0, nkc, score_body, 0)

    def rows_body(rt, carry):
        r0 = pl.multiple_of(rt * tr, tr)

        def count(pred):
            def cbody(c, part):
                kc = key_ref[c, pl.ds(r0, tr), :]
                for j in range(tk // LANES):
                    idx = c * tk + j * LANES + lax.broadcasted_iota(jnp.int32, (1, LANES), 1)
                    part = part + pred(kc[:, j * LANES:(j + 1) * LANES], idx).astype(jnp.int32)
                return part
            part = lax.fori_loop(0, nkc, cbody, jnp.zeros((tr, LANES), jnp.int32))
            return jnp.sum(part, axis=1, keepdims=True)

        def thr_bit(bi, t):
            cand = jnp.broadcast_to(t + jnp.left_shift(jnp.int32(1), 31 - bi), (tr, LANES))
            cnt = count(lambda k, idx: k >= cand)
            return jnp.where(cnt >= topk, cand[:, :1], t)

        thr = lax.fori_loop(0, 32, thr_bit, jnp.full((tr, 1), INT32_MIN, jnp.int32))
        thr_b = jnp.broadcast_to(thr, (tr, LANES))
        n_gt = count(lambda k, idx: k > thr_b)
        n_ge = count(lambda k, idx: k >= thr_b)
        need = topk - n_gt

        def tie_search():
            def tie_bit(bi, jbound):
                cand = jnp.broadcast_to(jbound + jnp.left_shift(jnp.int32(1), idx_bits - 1 - bi), (tr, LANES))
                before = count(lambda k, idx: (k == thr_b) & (idx < cand))
                return jnp.where(before <= need - 1, cand[:, :1], jbound)
            return lax.fori_loop(0, idx_bits, tie_bit, jnp.zeros((tr, 1), jnp.int32))

        tie = lax.cond(jnp.max(n_ge) > topk, tie_search,
                       lambda: jnp.full((tr, 1), 2 ** 31 - 1, jnp.int32))
        thr_ref[pl.ds(r0, tr), :] = thr_b
        tie_ref[pl.ds(r0, tr), :] = jnp.broadcast_to(tie, (tr, LANES))
        return carry

    lax.fori_loop(0, tq // tr, rows_body, 0)

    m_ref[...] = jnp.full(m_ref.shape, M_INIT, jnp.float32)
    l_ref[...] = jnp.zeros(l_ref.shape, jnp.float32)
    acc_ref[...] = jnp.zeros(acc_ref.shape, jnp.float32)
    thr_c = thr_ref[:, :1]
    tie_c = tie_ref[:, :1]

    def att_body(c, carry):
        kc = key_ref[c]
        idx = c * tk + lax.broadcasted_iota(jnp.int32, (1, tk), 1)
        sel = (kc > thr_c) | ((kc == thr_c) & (idx <= tie_c))
        bias = jnp.where(sel & (idx <= qpos), 0.0, MASK_BIAS)
        kt = kt_ref[0, c]
        vv = v_ref[0, c]
        for h in range(HC):
            s = jnp.dot(q_ref[0, :, h * DHC:(h + 1) * DHC], kt[h * DHC:(h + 1) * DHC, :],
                        preferred_element_type=jnp.float32) + bias
            _flash_update(s, vv[:, h * DHC:(h + 1) * DHC], m_ref, l_ref, acc_ref, h, DHC)
        return carry

    lax.fori_loop(0, nkc, att_body, 0)
    o_ref[0] = jnp.concatenate([acc_ref[h] / l_ref[h][:, :DHC] for h in range(HC)], axis=1)


def dsa_prompt_pallas(q, k, v, qi, wi, ki, *, tq=256, tk=256, tr=128):
    b, s, _ = q.shape
    topk = min(TOPK_MAX, s // 4)
    assert s % tq == 0 and tq % tk == 0 and tq % tr == 0 and tq >= topk
    nc = s // tk
    bf = jnp.bfloat16
    q_s = (q * (DHC ** -0.5)).astype(bf)
    kit = ki.astype(bf).reshape(b, nc, tk, DI).swapaxes(2, 3)
    kt = k.astype(bf).reshape(b, nc, tk, WC).swapaxes(2, 3)
    vc = v.astype(bf).reshape(b, nc, tk, WC)
    body = functools.partial(_dsa_prompt_body, tq=tq, tk=tk, tr=tr, topk=topk, idx_bits=int(s).bit_length())
    return pl.pallas_call(
        body,
        grid=(b, s // tq),
        in_specs=[
            pl.BlockSpec((1, tq, HI * DI), lambda bi, i: (bi, i, 0)),
            pl.BlockSpec((1, tq, HI), lambda bi, i: (bi, i, 0)),
            pl.BlockSpec((1, tq, WC), lambda bi, i: (bi, i, 0)),
            pl.BlockSpec((1, nc, DI, tk), lambda bi, i: (bi, 0, 0, 0)),
            pl.BlockSpec((1, nc, WC, tk), lambda bi, i: (bi, 0, 0, 0)),
            pl.BlockSpec((1, nc, tk, WC), lambda bi, i: (bi, 0, 0, 0)),
        ],
        out_specs=pl.BlockSpec((1, tq, WC), lambda bi, i: (bi, i, 0)),
        out_shape=jax.ShapeDtypeStruct((b, s, WC), jnp.float32),
        scratch_shapes=[
            pltpu.VMEM((nc, tq, tk), jnp.int32),
            pltpu.VMEM((HI, tq, LANES), jnp.float32),
            pltpu.VMEM((tq, LANES), jnp.int32),
            pltpu.VMEM((tq, LANES), jnp.int32),
            pltpu.VMEM((HC, tq, LANES), jnp.float32),
            pltpu.VMEM((HC, tq, LANES), jnp.float32),
            pltpu.VMEM((HC, tq, DHC), jnp.float32),
        ],
        compiler_params=pltpu.CompilerParams(dimension_semantics=("arbitrary", "arbitrary"),
                                             vmem_limit_bytes=VMEM_LIMIT_BYTES),
        name="dsa_prompt",
    )(qi.astype(bf), wi, q_s, kit, kt, vc)


def _diff_prompt_body(lam_ref, q_ref, kt_ref, v_ref, g_ref, o_ref, m_ref, l_ref, acc_ref,
                      *, tq, tk, out_scale):
    i = pl.program_id(1)
    ratio = tq // tk
    qpos = i * tq + lax.broadcasted_iota(jnp.int32, (tq, 1), 0)
    m_ref[...] = jnp.full(m_ref.shape, M_INIT, jnp.float32)
    l_ref[...] = jnp.zeros(l_ref.shape, jnp.float32)
    acc_ref[...] = jnp.zeros(acc_ref.shape, jnp.float32)

    def step(c, masked):
        kt = kt_ref[0, c]
        vv = v_ref[0, c]
        if masked:
            kpos = c * tk + lax.broadcasted_iota(jnp.int32, (1, tk), 1)
            bias = jnp.where(kpos <= qpos, 0.0, MASK_BIAS)
        for h in range(HB):
            for comp in range(2):
                o = (h * 2 + comp) * DHB
                s = jnp.dot(q_ref[0, :, o:o + DHB], kt[o:o + DHB, :], preferred_element_type=jnp.float32)
                if masked:
                    s = s + bias
                _flash_update(s, vv[:, h * DVB:(h + 1) * DVB], m_ref, l_ref, acc_ref, h * 2 + comp, DVB)

    def full_body(c, carry):
        step(c, False)
        return carry

    def diag_body(c, carry):
        step(c, True)
        return carry

    lax.fori_loop(0, i * ratio, full_body, 0)
    lax.fori_loop(i * ratio, (i + 1) * ratio, diag_body, 0)

    lam = lam_ref[0]
    outs = []
    for h in range(HB):
        o1 = acc_ref[2 * h] / l_ref[2 * h]
        o2 = acc_ref[2 * h + 1] / l_ref[2 * h + 1]
        o = o1 - lam * o2
        o = o * lax.rsqrt(jnp.mean(o * o, axis=1, keepdims=True) + DIFF_SUBLN_EPS)
        outs.append(o * g_ref[...] * out_scale)
    o_ref[0] = jnp.concatenate(outs, axis=1)


def diff_prompt_pallas(q, k, v, lam, subln_g, out_scale, *, tq=512, tk=512):
    b, s, _ = q.shape
    tq, tk = min(tq, s), min(tk, s)
    assert s % tq == 0 and tq % tk == 0 and DVB == LANES
    nc = s // tk
    bf = jnp.bfloat16
    wq = HB * 2 * DHB
    q_s = (q * (DHB ** -0.5)).astype(bf)
    kt = k.astype(bf).reshape(b, nc, tk, wq).swapaxes(2, 3)
    vc = v.astype(bf).reshape(b, nc, tk, WB)
    body = functools.partial(_diff_prompt_body, tq=tq, tk=tk, out_scale=out_scale)
    return pl.pallas_call(
        body,
        grid=(b, s // tq),
        in_specs=[
            pl.BlockSpec(memory_space=pltpu.SMEM),
            pl.BlockSpec((1, tq, wq), lambda bi, i: (bi, i, 0)),
            pl.BlockSpec((1, nc, wq, tk), lambda bi, i: (bi, 0, 0, 0)),
            pl.BlockSpec((1, nc, tk, WB), lambda bi, i: (bi, 0, 0, 0)),
            pl.BlockSpec((1, DVB), lambda bi, i: (0, 0)),
        ],
        out_specs=pl.BlockSpec((1, tq, WB), lambda bi, i: (bi, i, 0)),
        out_shape=jax.ShapeDtypeStruct((b, s, WB), jnp.float32),
        scratch_shapes=[
            pltpu.VMEM((HB * 2, tq, LANES), jnp.float32),
            pltpu.VMEM((HB * 2, tq, LANES), jnp.float32),
            pltpu.VMEM((HB * 2, tq, DVB), jnp.float32),
        ],
        compiler_params=pltpu.CompilerParams(dimension_semantics=("arbitrary", "arbitrary"),
                                             vmem_limit_bytes=VMEM_LIMIT_BYTES),
        name="diff_prompt",
    )(lam.reshape(1).astype(jnp.float32), q_s, kt, vc, subln_g.reshape(1, DVB).astype(jnp.float32))


RWKV_TBLK = LANES // 2
RWKV_PAIRS = HA // 2


def _rwkv_scan_body(r_ref, w_ref, k_ref, kk_ref, b_ref, vt_ref, e_ref, s0_ref, ot_ref, sout_ref,
                    s_scr, vb_scr, sr_scr, *, bb, nsteps):
    ti = pl.program_id(1)

    @pl.when(ti == 0)
    def _():
        s_scr[...] = s0_ref[...]

    slot = lax.broadcasted_iota(jnp.int32, (1, LANES), 1) & (RWKV_TBLK - 1)
    e = e_ref[...]
    bf = jnp.bfloat16
    units = [(bi, p) for bi in range(bb) for p in range(RWKV_PAIRS)]
    n_u = len(units)

    for u, (bi, p) in enumerate(units):
        vt = vt_ref[bi, p, 0]
        lhs = jnp.concatenate([jnp.where(slot == t, vt, 0.0).astype(bf) for t in range(nsteps)], axis=0)
        vb_scr[u] = jnp.dot(lhs, e, preferred_element_type=jnp.float32).reshape(nsteps, NA, LANES)

    def step(t, carry):
        rows = [[ref[bi, pl.ds(t, 1), :] for ref in (r_ref, w_ref, k_ref, kk_ref, b_ref)] for bi in range(bb)]
        ms = []
        for bi, p in units:
            ms.append(s_scr[bi, p] * rows[bi][3][:, p * LANES:(p + 1) * LANES])
        m_hi = [m.astype(bf) for m in ms]
        m_lo = [(m - h.astype(jnp.float32)).astype(bf) for m, h in zip(ms, m_hi)]
        sk_all = jnp.dot(jnp.concatenate(m_hi + m_lo, axis=0), e, preferred_element_type=jnp.float32)
        for u, (bi, p) in enumerate(units):
            cols = slice(p * LANES, (p + 1) * LANES)
            r_t, w_t, k_t, _, b_t = rows[bi]
            sk = sk_all[u * NA:(u + 1) * NA] + sk_all[(n_u + u) * NA:(n_u + u + 1) * NA]
            s = s_scr[bi, p] * w_t[:, cols] - sk * b_t[:, cols] + vb_scr[u, t] * k_t[:, cols]
            s_scr[bi, p] = s
            sr_scr[u, t] = s * r_t[:, cols]
        return carry

    lax.fori_loop(0, nsteps, step, 0)

    for u, (bi, p) in enumerate(units):
        o_b = jnp.dot(sr_scr[u].reshape(nsteps * NA, LANES).astype(bf), e, preferred_element_type=jnp.float32)
        ot = jnp.zeros((NA, LANES), jnp.float32)
        for t in range(nsteps):
            ot = jnp.where(slot == t, o_b[t * NA:(t + 1) * NA], ot)
        ot_ref[bi, p, 0] = ot

    @pl.when(ti == pl.num_programs(1) - 1)
    def _():
        sout_ref[...] = s_scr[...]


def rwkv_scan_pallas(r, w, k, v, kk, b, s0, *, bb=2):
    nb, t, _ = r.shape
    nsteps = min(RWKV_TBLK, t)
    assert nb % bb == 0 and t % nsteps == 0 and LANES == 2 * NA
    nblk = t // nsteps
    vt = v.reshape(nb, nblk, nsteps, RWKV_PAIRS, 2, NA).transpose(0, 3, 1, 5, 4, 2)
    vt = jnp.pad(vt, ((0, 0),) * 5 + ((0, RWKV_TBLK - nsteps),)).reshape(nb, RWKV_PAIRS, nblk, NA, LANES)
    s0p = s0.reshape(nb, RWKV_PAIRS, 2, NA, NA).transpose(0, 1, 3, 2, 4).reshape(nb, RWKV_PAIRS, NA, LANES)
    head_of_lane = jnp.arange(LANES) // NA
    e = (head_of_lane[:, None] == head_of_lane[None, :]).astype(jnp.bfloat16)
    row_spec = pl.BlockSpec((bb, nsteps, WA), lambda bi, ti: (bi, ti, 0))
    col_spec = pl.BlockSpec((bb, RWKV_PAIRS, 1, NA, LANES), lambda bi, ti: (bi, 0, ti, 0, 0))
    st_spec = pl.BlockSpec((bb, RWKV_PAIRS, NA, LANES), lambda bi, ti: (bi, 0, 0, 0))
    ot, s_fin = pl.pallas_call(
        functools.partial(_rwkv_scan_body, bb=bb, nsteps=nsteps),
        grid=(nb // bb, nblk),
        in_specs=[row_spec, row_spec, row_spec, row_spec, row_spec, col_spec,
                  pl.BlockSpec((LANES, LANES), lambda bi, ti: (0, 0)), st_spec],
        out_specs=[col_spec, st_spec],
        out_shape=[jax.ShapeDtypeStruct((nb, RWKV_PAIRS, nblk, NA, LANES), jnp.float32),
                   jax.ShapeDtypeStruct((nb, RWKV_PAIRS, NA, LANES), jnp.float32)],
        scratch_shapes=[pltpu.VMEM((bb, RWKV_PAIRS, NA, LANES), jnp.float32),
                        pltpu.VMEM((bb * RWKV_PAIRS, nsteps, NA, LANES), jnp.float32),
                        pltpu.VMEM((bb * RWKV_PAIRS, nsteps, NA, LANES), jnp.float32)],
        compiler_params=pltpu.CompilerParams(dimension_semantics=("arbitrary", "arbitrary"),
                                             vmem_limit_bytes=VMEM_LIMIT_BYTES),
        name="rwkv_scan",
    )(r, w, k, kk, b, vt, e, s0p)
    o = ot.reshape(nb, RWKV_PAIRS, nblk, NA, 2, RWKV_TBLK)[..., :nsteps]
    o = o.transpose(0, 2, 5, 1, 4, 3).reshape(nb, t, WA)
    s_fin = s_fin.reshape(nb, RWKV_PAIRS, NA, 2, NA).transpose(0, 1, 3, 2, 4).reshape(nb, HA, NA, NA)
    return o, s_fin


def split_last(z, sizes):
    offs = []
    acc = 0
    for s in sizes[:-1]:
        acc += s
        offs.append(acc)
    return jnp.split(z, offs, axis=-1)


def rmsnorm(x, g, eps=NORM_EPS):
    xf = x.astype(jnp.float32)
    y = xf * lax.rsqrt(jnp.mean(xf * xf, axis=-1, keepdims=True) + eps)
    return (y * g.astype(jnp.float32)).astype(x.dtype)


def to_blocks(z):
    b, s = z.shape[:2]
    return z.reshape(b, s // QBLOCK, QBLOCK, *z.shape[2:]).swapaxes(0, 1)


def from_blocks(z):
    nb, b = z.shape[:2]
    return z.swapaxes(0, 1).reshape(b, nb * QBLOCK, *z.shape[3:])


def rwkv7_mix(pa, prev, s0, mu, w0, w_up, a0, a_up, g_up, k_k, k_a, r_k, ln_w, ln_b):
    B, T, _ = pa.shape
    f = lambda z: z.astype(jnp.float32)
    p = f(pa)
    shifted = jnp.concatenate([f(prev)[:, None], p[:, :-1]], axis=1)
    px = p + (shifted - p) * f(mu)
    r, k, v, wd, ad, gd = split_last(px, [WA, WA, WA, LORA_W, LORA_A, LORA_G])
    w = -jax.nn.softplus(-(f(w0) + jnp.tanh(wd) @ f(w_up))) - 0.5
    decay = jnp.exp(-jnp.exp(w))
    a = jax.nn.sigmoid(f(a0) + ad @ f(a_up))
    g = jax.nn.sigmoid(gd) @ f(g_up)
    hd = lambda z: z.reshape(B, T, HA, NA)
    kk = hd(k * f(k_k))
    kk = kk * lax.rsqrt(jnp.maximum(jnp.sum(kk * kk, -1, keepdims=True), 1e-24))
    k = k * (1.0 + (a - 1.0) * f(k_a))
    kk_flat = kk.reshape(B, T, WA)
    o, s_fin = rwkv_scan_pallas(r, decay, k, v, kk_flat, kk_flat * a, f(s0))
    o = hd(o)
    r, k, v = hd(r), hd(k), hd(v)
    mean = jnp.mean(o, -1, keepdims=True)
    var = jnp.mean(jnp.square(o - mean), -1, keepdims=True)
    o = ((o - mean) * lax.rsqrt(var + RWKV_GN_EPS)).reshape(B, T, WA) * f(ln_w) + f(ln_b)
    bonus = jnp.sum(r * k * f(r_k), -1, keepdims=True) * v
    o = (o + bonus.reshape(B, T, WA)) * g
    return o, s_fin.astype(s0.dtype), pa[:, -1]


def diff_attend(q, k, v, mask, lam):
    s = jnp.einsum('bqhcd,bkhcd->bhcqk', q.astype(jnp.float32), k.astype(jnp.float32)) * (DHB ** -0.5)
    p = jax.nn.softmax(jnp.where(mask, s, -jnp.inf), axis=-1)
    pd = p[:, :, 0] - lam * p[:, :, 1]
    return jnp.einsum('bhqk,bkhd->bqhd', pd, v.astype(jnp.float32))


def diff_prompt(q, k, v, lam):
    S = q.shape[1]
    kpos = jnp.arange(S)

    def block(args):
        q_i, i = args
        qpos = i * QBLOCK + jnp.arange(QBLOCK)
        return diff_attend(q_i, k, v, qpos[:, None] >= kpos[None, :], lam)

    return from_blocks(lax.map(block, (to_blocks(q), jnp.arange(S // QBLOCK))))


def diff_sample(q, k_new, v_new, pool_k, pool_v, page_table, l, lam):
    T = q.shape[1]
    past = page_table.shape[1] * PAGE_SIZE
    mask = jnp.concatenate([jnp.ones((T, past), bool), jnp.tril(jnp.ones((T, T), bool))], axis=1)

    def one(args):
        q_b, kn_b, vn_b, pt_b = args
        kp = pool_k[l, pt_b].reshape(past, HB, 2, DHB)
        vp = pool_v[l, pt_b].reshape(past, HB, DVB)
        kc = jnp.concatenate([kp, kn_b], axis=0)[None]
        vc = jnp.concatenate([vp, vn_b], axis=0)[None]
        return diff_attend(q_b[None], kc, vc, mask, lam)[0]

    return lax.map(one, (q, k_new, v_new, page_table))


def indexer_scores(qi, wi, ki):
    qk = jnp.einsum('bthd,bsd->bths', qi.astype(jnp.float32), ki.astype(jnp.float32))
    return jnp.einsum('bth,bths->bts', wi.astype(jnp.float32), jax.nn.relu(qk))


def sparse_attend(q, k_sel, v_sel, valid):
    s = jnp.einsum('bthd,btkhd->bthk', q.astype(jnp.float32), k_sel.astype(jnp.float32)) * (DHC ** -0.5)
    p = jax.nn.softmax(jnp.where(valid[:, :, None, :], s, -jnp.inf), axis=-1)
    return jnp.einsum('bthk,btkhd->bthd', p, v_sel.astype(jnp.float32))


def dsa_prompt(q, k, v, qi, wi, ki):
    B, S = q.shape[:2]
    topk = min(TOPK_MAX, S // 4)
    kpos = jnp.arange(S)
    b_idx = jnp.arange(B)[:, None, None]

    def block(args):
        q_i, qi_i, wi_i, i = args
        qpos = i * QBLOCK + jnp.arange(QBLOCK)
        sc = jnp.where(qpos[:, None] >= kpos[None, :], indexer_scores(qi_i, wi_i, ki), -jnp.inf)
        _, sel = lax.top_k(sc, topk)
        return sparse_attend(q_i, k[b_idx, sel], v[b_idx, sel], sel <= qpos[None, :, None])

    o = lax.map(block, (to_blocks(q), to_blocks(qi), to_blocks(wi), jnp.arange(S // QBLOCK)))
    return from_blocks(o)


def dsa_sample(q, k_new, v_new, qi, wi, ki_new, pool_k, pool_v, pool_ki, page_table, l):
    DB, T = q.shape[:2]
    past = page_table.shape[1] * PAGE_SIZE
    L = past + T
    topk = min(TOPK_MAX, L // 4)
    ki_all = jnp.concatenate([pool_ki[l, page_table].reshape(DB, past, DI), ki_new], axis=1)
    qpos = past + jnp.arange(T)
    sc = jnp.where(qpos[:, None] >= jnp.arange(L)[None, :], indexer_scores(qi, wi, ki_all), -jnp.inf)
    _, sel = lax.top_k(sc, topk)
    b_idx = jnp.arange(DB)[:, None, None]
    from_past = (sel < past)[..., None, None]
    s_p = jnp.minimum(sel, past - 1)
    phys = page_table[b_idx, s_p // PAGE_SIZE]
    off = s_p % PAGE_SIZE
    s_n = jnp.clip(sel - past, 0, T - 1)
    k_sel = jnp.where(from_past, pool_k[l, phys, off], k_new[b_idx, s_n])
    v_sel = jnp.where(from_past, pool_v[l, phys, off], v_new[b_idx, s_n])
    return sparse_attend(q, k_sel, v_sel, sel <= qpos[None, :, None])


def kernel(x_prompt, x_sample, state_rwkv, state_shift, cache_diff_k, cache_diff_v, cache_dsa_k,
           cache_dsa_v, cache_idx_k, page_table, g_mix, w_in, rwkv_mu, rwkv_w0, rwkv_w_up, rwkv_a0,
           rwkv_a_up, rwkv_g_up, rwkv_k_k, rwkv_k_a, rwkv_r_k, rwkv_ln_w, rwkv_ln_b, diff_lam_q1,
           diff_lam_k1, diff_lam_q2, diff_lam_k2, diff_subln_g, w_br_a, w_br_b, w_br_c, w_out, g_ffn,
           w_ffn_gate, w_ffn_up, w_ffn_down, g_final):
    B = x_prompt.shape[0]

    def project(h, l):
        b, t = h.shape[:2]
        xn = rmsnorm(h, g_mix[l])
        pa, pb, pc, pg = split_last(xn @ w_in[l], [NCA, NCB, NCC, N_BRANCH * D_MODEL])
        qb, kb, vb = split_last(pb, [HB * 2 * DHB, HB * 2 * DHB, WB])
        qc, kc, vc, qi, ki, wi = split_last(pc, [WC, WC, WC, HI * DI, DI, HI])
        diff = (qb.reshape(b, t, HB, 2, DHB), kb.reshape(b, t, HB, 2, DHB), vb.reshape(b, t, HB, DVB))
        dsa = (qc.reshape(b, t, HC, DHC), kc.reshape(b, t, HC, DHC), vc.reshape(b, t, HC, DHC),
               qi.reshape(b, t, HI, DI) * (DI ** -0.5), wi * (HI ** -0.5), ki)
        gates = jax.nn.sigmoid(pg).reshape(b, t, N_BRANCH, D_MODEL)
        return pa, diff, dsa, gates

    def finish(h, oa, ob, oc, gates, l, lam_init):
        b, t = h.shape[:2]
        br_a = oa.astype(h.dtype) @ w_br_a[l]
        br_b = ob.reshape(b, t, WB).astype(h.dtype) @ w_br_b[l]
        br_c = oc.reshape(b, t, WC).astype(h.dtype) @ w_br_c[l]
        merged = gates[:, :, 0] * br_a + gates[:, :, 1] * br_b + gates[:, :, 2] * br_c
        h = h + merged @ w_out[l]
        xn = rmsnorm(h, g_ffn[l])
        return h + (jax.nn.silu(xn @ w_ffn_gate[l]) * (xn @ w_ffn_up[l])) @ w_ffn_down[l]

    names = ('rwkv', 'shift', 'dk', 'dv', 'ck', 'cv', 'ik')
    newp = {n: [] for n in names}
    news = {n: [] for n in names}
    hp, hs = x_prompt, x_sample
    for l in range(DEPTH):
        lam_init = 0.8 - 0.6 * math.exp(-0.3 * l)
        lam = (jnp.exp(jnp.sum(diff_lam_q1[l] * diff_lam_k1[l])) - jnp.exp(jnp.sum(diff_lam_q2[l] * diff_lam_k2[l]))
               + lam_init).astype(jnp.float32)
        rw = (rwkv_mu[l], rwkv_w0[l], rwkv_w_up[l], rwkv_a0[l], rwkv_a_up[l], rwkv_g_up[l], rwkv_k_k[l],
              rwkv_k_a[l], rwkv_r_k[l], rwkv_ln_w[l], rwkv_ln_b[l])

        pa, (qb, kb, vb), (qc, kc, vc, qi, wi, ki), gates = project(hp, l)
        oa, s_a, sh_a = rwkv7_mix(pa, jnp.zeros((B, NCA), pa.dtype),
                                  jnp.zeros((B, HA, NA, NA), state_rwkv.dtype), *rw)
        bp, tp = hp.shape[:2]
        ob = diff_prompt_pallas(qb.reshape(bp, tp, -1), kb.reshape(bp, tp, -1), vb.reshape(bp, tp, -1),
                                lam, diff_subln_g[l], 1.0 - lam_init)
        oc = dsa_prompt_pallas(qc.reshape(bp, tp, -1), kc.reshape(bp, tp, -1), vc.reshape(bp, tp, -1),
                               qi.reshape(bp, tp, -1), wi, ki)
        hp = finish(hp, oa, ob, oc, gates, l, lam_init)
        for n, val in zip(names, (s_a, sh_a, kb, vb, kc, vc, ki)):
            newp[n].append(val)

        pa, (qb, kb, vb), (qc, kc, vc, qi, wi, ki), gates = project(hs, l)
        oa, s_a, sh_a = rwkv7_mix(pa, state_shift[l], state_rwkv[l], *rw)
        ob = diff_sample(qb, kb, vb, cache_diff_k, cache_diff_v, page_table, l, lam)
        ob = rmsnorm(ob, diff_subln_g[l], DIFF_SUBLN_EPS) * (1.0 - lam_init)
        oc = dsa_sample(qc, kc, vc, qi, wi, ki, cache_dsa_k, cache_dsa_v, cache_idx_k, page_table, l)
        hs = finish(hs, oa, ob, oc, gates, l, lam_init)
        for n, val in zip(names, (s_a, sh_a, kb, vb, kc, vc, ki)):
            news[n].append(val)

    yp = rmsnorm_pallas(hp.reshape(-1, D_MODEL), g_final).reshape(hp.shape)
    ys = rmsnorm_pallas(hs.reshape(-1, D_MODEL), g_final).reshape(hs.shape)
    return (yp, ys,
            jnp.stack(newp['rwkv']), jnp.stack(newp['shift']), jnp.stack(newp['dk']), jnp.stack(newp['dv']),
            jnp.stack(newp['ck']), jnp.stack(newp['cv']), jnp.stack(newp['ik']),
            jnp.stack(news['rwkv']), jnp.stack(news['shift']), jnp.stack(news['dk']), jnp.stack(news['dv']),
            jnp.stack(news['ck']), jnp.stack(news['cv']), jnp.stack(news['ik']))
```

```python
import functools
import math
import jax
import jax.numpy as jnp
from jax import lax
import numpy as np
from jax.experimental import pallas as pl
from jax.experimental.pallas import tpu as pltpu

D_MODEL = 1024
BATCH = 2
SEQ = 8192
DEPTH = 2
DEC_BATCH = 32
DEC_SEQ = 8
PAST_LEN = 16384
PAGE_SIZE = 128

HA = 4
NA = 64
WA = HA * NA
LORA_W = 64
LORA_A = 64
LORA_G = 128
NCA = 3 * WA + LORA_W + LORA_A + LORA_G
RWKV_GN_EPS = 64e-5
HB = 4
DHB = 64
DVB = 2 * DHB
WB = HB * DVB
NCB = 2 * (HB * 2 * DHB) + WB
DIFF_SUBLN_EPS = 1e-5
HC = 4
DHC = 64
WC = HC * DHC
HI = 8
DI = 64
TOPK_MAX = 256
NCC = 3 * WC + HI * DI + DI + HI
N_BRANCH = 3
N_IN = NCA + NCB + NCC + N_BRANCH * D_MODEL
D_FF = -(-8 * D_MODEL // (3 * 256)) * 256
QBLOCK = 128
NORM_EPS = 1e-6


def _rmsnorm_body(x_ref, g_ref, o_ref, *, eps):
    x = x_ref[...]
    y = x * lax.rsqrt(jnp.mean(x * x, axis=-1, keepdims=True) + eps)
    o_ref[...] = y * g_ref[...]


def rmsnorm_pallas(x2d, g, eps=NORM_EPS, tm=256):
    t, d = x2d.shape
    return pl.pallas_call(
        functools.partial(_rmsnorm_body, eps=eps),
        grid=(t // tm,),
        in_specs=[pl.BlockSpec((tm, d), lambda i: (i, 0)), pl.BlockSpec((1, d), lambda i: (0, 0))],
        out_specs=pl.BlockSpec((tm, d), lambda i: (i, 0)),
        out_shape=jax.ShapeDtypeStruct((t, d), jnp.float32),
        name="final_rmsnorm",
    )(x2d, g.reshape(1, d))


VMEM_LIMIT_BYTES = 56 * 1024 * 1024
LANES = 128
MASK_BIAS = -1e30
M_INIT = -0.5e30
INT32_MIN = -(2 ** 31)


def _flash_update(s, v_bf16, m_ref, l_ref, acc_ref, slot, dv):
    m_prev = m_ref[slot]
    m_new = jnp.maximum(m_prev, jnp.max(s, axis=1, keepdims=True))
    alpha = jnp.exp(m_prev - m_new)
    p = jnp.exp(s - m_new[:, :1])
    l_ref[slot] = alpha * l_ref[slot] + jnp.sum(p, axis=1, keepdims=True)
    pv = jnp.dot(p.astype(jnp.bfloat16), v_bf16, preferred_element_type=jnp.float32)
    acc_ref[slot] = acc_ref[slot] * alpha[:, :dv] + pv
    m_ref[slot] = m_new


def _dsa_prompt_body(qi_ref, wi_ref, q_ref, kit_ref, kt_ref, v_ref, o_ref,
                     key_ref, wb_ref, thr_ref, tie_ref, m_ref, l_ref, acc_ref, *, tq, tk, tr, topk, idx_bits):
    i = pl.program_id(1)
    nkc = (i + 1) * (tq // tk)
    qpos = i * tq + lax.broadcasted_iota(jnp.int32, (tq, 1), 0)

    for h in range(HI):
        wb_ref[h] = jnp.broadcast_to(wi_ref[0, :, h:h + 1], (tq, LANES))

    def score_body(c, carry):
        kit = kit_ref[0, c]
        tiles = [jnp.zeros((tq, LANES), jnp.float32)] * (tk // LANES)
        for h in range(HI):
            qk = jnp.dot(qi_ref[0, :, h * DI:(h + 1) * DI], kit, preferred_element_type=jnp.float32)
            w_b = wb_ref[h]
            tiles = [a + w_b * jnp.maximum(qk[:, j * LANES:(j + 1) * LANES], 0.0) for j, a in enumerate(tiles)]
        acc = jnp.concatenate(tiles, axis=1)
        acc = jnp.where(acc == 0.0, 0.0, acc)
        kpos = c * tk + lax.broadcasted_iota(jnp.int32, (1, tk), 1)
        acc = jnp.where(kpos <= qpos, acc, -jnp.inf)
        bits = pltpu.bitcast(acc, jnp.int32)
        key_ref[c] = bits ^ ((bits >> 31) & 0x7FFFFFFF)
        return carry

    lax.fori_loop(0, nkc, score_body, 0)

    def rows_body(rt, carry):
        r0 = pl.multiple_of(rt * tr, tr)

        def count(pred):
            def cbody(c, part):
                kc = key_ref[c, pl.ds(r0, tr), :]
                for j in range(tk // LANES):
                    idx = c * tk + j * LANES + lax.broadcasted_iota(jnp.int32, (1, LANES), 1)
                    part = part + pred(kc[:, j * LANES:(j + 1) * LANES], idx).astype(jnp.int32)
                return part
            part = lax.fori_loop(0, nkc, cbody, jnp.zeros((tr, LANES), jnp.int32))
            return jnp.sum(part, axis=1, keepdims=True)

        def thr_bit(bi, t):
            cand = jnp.broadcast_to(t + jnp.left_shift(jnp.int32(1), 31 - bi), (tr, LANES))
            cnt = count(lambda k, idx: k >= cand)
            return jnp.where(cnt >= topk, cand[:, :1], t)

        thr = lax.fori_loop(0, 32, thr_bit, jnp.full((tr, 1), INT32_MIN, jnp.int32))
        thr_b = jnp.broadcast_to(thr, (tr, LANES))
        n_gt = count(lambda k, idx: k > thr_b)
        n_ge = count(lambda k, idx: k >= thr_b)
        need = topk - n_gt

        def tie_search():
            def tie_bit(bi, jbound):
                cand = jnp.broadcast_to(jbound + jnp.left_shift(jnp.int32(1), idx_bits - 1 - bi), (tr, LANES))
                before = count(lambda k, idx: (k == thr_b) & (idx < cand))
                return jnp.where(before <= need - 1, cand[:, :1], jbound)
            return lax.fori_loop(0, idx_bits, tie_bit, jnp.zeros((tr, 1), jnp.int32))

        tie = lax.cond(jnp.max(n_ge) > topk, tie_search,
                       lambda: jnp.full((tr, 1), 2 ** 31 - 1, jnp.int32))
        thr_ref[pl.ds(r0, tr), :] = thr_b
        tie_ref[pl.ds(r0, tr), :] = jnp.broadcast_to(tie, (tr, LANES))
        return carry

    lax.fori_loop(0, tq // tr, rows_body, 0)

    m_ref[...] = jnp.full(m_ref.shape, M_INIT, jnp.float32)
    l_ref[...] = jnp.zeros(l_ref.shape, jnp.float32)
    acc_ref[...] = jnp.zeros(acc_ref.shape, jnp.float32)
    thr_c = thr_ref[:, :1]
    tie_c = tie_ref[:, :1]

    def att_body(c, carry):
        kc = key_ref[c]
        idx = c * tk + lax.broadcasted_iota(jnp.int32, (1, tk), 1)
        sel = (kc > thr_c) | ((kc == thr_c) & (idx <= tie_c))
        bias = jnp.where(sel & (idx <= qpos), 0.0, MASK_BIAS)
        kt = kt_ref[0, c]
        vv = v_ref[0, c]
        for h in range(HC):
            s = jnp.dot(q_ref[0, :, h * DHC:(h + 1) * DHC], kt[h * DHC:(h + 1) * DHC, :],
                        preferred_element_type=jnp.float32) + bias
            _flash_update(s, vv[:, h * DHC:(h + 1) * DHC], m_ref, l_ref, acc_ref, h, DHC)
        return carry

    lax.fori_loop(0, nkc, att_body, 0)
    o_ref[0] = jnp.concatenate([acc_ref[h] / l_ref[h][:, :DHC] for h in range(HC)], axis=1)


def dsa_prompt_pallas(q, k, v, qi, wi, ki, *, tq=256, tk=256, tr=128):
    b, s, _ = q.shape
    topk = min(TOPK_MAX, s // 4)
    assert s % tq == 0 and tq % tk == 0 and tq % tr == 0 and tq >= topk
    nc = s // tk
    bf = jnp.bfloat16
    q_s = (q * (DHC ** -0.5)).astype(bf)
    kit = ki.astype(bf).reshape(b, nc, tk, DI).swapaxes(2, 3)
    kt = k.astype(bf).reshape(b, nc, tk, WC).swapaxes(2, 3)
    vc = v.astype(bf).reshape(b, nc, tk, WC)
    body = functools.partial(_dsa_prompt_body, tq=tq, tk=tk, tr=tr, topk=topk, idx_bits=int(s).bit_length())
    return pl.pallas_call(
        body,
        grid=(b, s // tq),
        in_specs=[
            pl.BlockSpec((1, tq, HI * DI), lambda bi, i: (bi, i, 0)),
            pl.BlockSpec((1, tq, HI), lambda bi, i: (bi, i, 0)),
            pl.BlockSpec((1, tq, WC), lambda bi, i: (bi, i, 0)),
            pl.BlockSpec((1, nc, DI, tk), lambda bi, i: (bi, 0, 0, 0)),
            pl.BlockSpec((1, nc, WC, tk), lambda bi, i: (bi, 0, 0, 0)),
            pl.BlockSpec((1, nc, tk, WC), lambda bi, i: (bi, 0, 0, 0)),
        ],
        out_specs=pl.BlockSpec((1, tq, WC), lambda bi, i: (bi, i, 0)),
        out_shape=jax.ShapeDtypeStruct((b, s, WC), jnp.float32),
        scratch_shapes=[
            pltpu.VMEM((nc, tq, tk), jnp.int32),
            pltpu.VMEM((HI, tq, LANES), jnp.float32),
            pltpu.VMEM((tq, LANES), jnp.int32),
            pltpu.VMEM((tq, LANES), jnp.int32),
            pltpu.VMEM((HC, tq, LANES), jnp.float32),
            pltpu.VMEM((HC, tq, LANES), jnp.float32),
            pltpu.VMEM((HC, tq, DHC), jnp.float32),
        ],
        compiler_params=pltpu.CompilerParams(dimension_semantics=("arbitrary", "arbitrary"),
                                             vmem_limit_bytes=VMEM_LIMIT_BYTES),
        name="dsa_prompt",
    )(qi.astype(bf), wi, q_s, kit, kt, vc)


def _diff_prompt_body(lam_ref, q_ref, kt_ref, v_ref, g_ref, o_ref, m_ref, l_ref, acc_ref,
                      *, tq, tk, out_scale):
    i = pl.program_id(1)
    ratio = tq // tk
    qpos = i * tq + lax.broadcasted_iota(jnp.int32, (tq, 1), 0)
    m_ref[...] = jnp.full(m_ref.shape, M_INIT, jnp.float32)
    l_ref[...] = jnp.zeros(l_ref.shape, jnp.float32)
    acc_ref[...] = jnp.zeros(acc_ref.shape, jnp.float32)

    def step(c, masked):
        kt = kt_ref[0, c]
        vv = v_ref[0, c]
        if masked:
            kpos = c * tk + lax.broadcasted_iota(jnp.int32, (1, tk), 1)
            bias = jnp.where(kpos <= qpos, 0.0, MASK_BIAS)
        for h in range(HB):
            for comp in range(2):
                o = (h * 2 + comp) * DHB
                s = jnp.dot(q_ref[0, :, o:o + DHB], kt[o:o + DHB, :], preferred_element_type=jnp.float32)
                if masked:
                    s = s + bias
                _flash_update(s, vv[:, h * DVB:(h + 1) * DVB], m_ref, l_ref, acc_ref, h * 2 + comp, DVB)

    def full_body(c, carry):
        step(c, False)
        return carry

    def diag_body(c, carry):
        step(c, True)
        return carry

    lax.fori_loop(0, i * ratio, full_body, 0)
    lax.fori_loop(i * ratio, (i + 1) * ratio, diag_body, 0)

    lam = lam_ref[0]
    outs = []
    for h in range(HB):
        o1 = acc_ref[2 * h] / l_ref[2 * h]
        o2 = acc_ref[2 * h + 1] / l_ref[2 * h + 1]
        o = o1 - lam * o2
        o = o * lax.rsqrt(jnp.mean(o * o, axis=1, keepdims=True) + DIFF_SUBLN_EPS)
        outs.append(o * g_ref[...] * out_scale)
    o_ref[0] = jnp.concatenate(outs, axis=1)


def diff_prompt_pallas(q, k, v, lam, subln_g, out_scale, *, tq=512, tk=512):
    b, s, _ = q.shape
    tq, tk = min(tq, s), min(tk, s)
    assert s % tq == 0 and tq % tk == 0 and DVB == LANES
    nc = s // tk
    bf = jnp.bfloat16
    wq = HB * 2 * DHB
    q_s = (q * (DHB ** -0.5)).astype(bf)
    kt = k.astype(bf).reshape(b, nc, tk, wq).swapaxes(2, 3)
    vc = v.astype(bf).reshape(b, nc, tk, WB)
    body = functools.partial(_diff_prompt_body, tq=tq, tk=tk, out_scale=out_scale)
    return pl.pallas_call(
        body,
        grid=(b, s // tq),
        in_specs=[
            pl.BlockSpec(memory_space=pltpu.SMEM),
            pl.BlockSpec((1, tq, wq), lambda bi, i: (bi, i, 0)),
            pl.BlockSpec((1, nc, wq, tk), lambda bi, i: (bi, 0, 0, 0)),
            pl.BlockSpec((1, nc, tk, WB), lambda bi, i: (bi, 0, 0, 0)),
            pl.BlockSpec((1, DVB), lambda bi, i: (0, 0)),
        ],
        out_specs=pl.BlockSpec((1, tq, WB), lambda bi, i: (bi, i, 0)),
        out_shape=jax.ShapeDtypeStruct((b, s, WB), jnp.float32),
        scratch_shapes=[
            pltpu.VMEM((HB * 2, tq, LANES), jnp.float32),
            pltpu.VMEM((HB * 2, tq, LANES), jnp.float32),
            pltpu.VMEM((HB * 2, tq, DVB), jnp.float32),
        ],
        compiler_params=pltpu.CompilerParams(dimension_semantics=("arbitrary", "arbitrary"),
                                             vmem_limit_bytes=VMEM_LIMIT_BYTES),
        name="diff_prompt",
    )(lam.reshape(1).astype(jnp.float32), q_s, kt, vc, subln_g.reshape(1, DVB).astype(jnp.float32))


PAGES_PER_STEP = 8


def _order_key(x):
    x = jnp.where(x == 0.0, 0.0, x)
    bits = pltpu.bitcast(x, jnp.int32)
    return bits ^ ((bits >> 31) & 0x7FFFFFFF)


def _page_specs(block, layer, n_rep):
    def spec(g):
        return pl.BlockSpec((None, None) + block,
                            lambda b, j, pt: (layer, pt[b, j * n_rep + g], 0, 0))
    return [spec(g) for g in range(n_rep)]


def _dsa_sample_select_body(pt_ref, qi_ref, w_ref, kin_ref, *rest, pg, nj, t_new, topk, idx_bits):
    page_refs, (bias_ref, key_scr, keyn_scr) = rest[:pg], rest[pg:]
    j = pl.program_id(1)
    width = pg * PAGE_SIZE
    bf = jnp.bfloat16

    def scores(kit):
        qk = jnp.dot(qi_ref[0], kit.astype(bf), preferred_element_type=jnp.float32)
        sc = jnp.maximum(qk, 0.0) * w_ref[0]
        out = sc[0:t_new]
        for h in range(1, HI):
            out = out + sc[h * t_new:(h + 1) * t_new]
        return out

    key_scr[j] = _order_key(scores(jnp.concatenate([r[...] for r in page_refs], axis=1)))

    @pl.when(j == nj - 1)
    def _():
        past = nj * width
        tok = lax.broadcasted_iota(jnp.int32, (t_new, 1), 0)
        lane_n = lax.broadcasted_iota(jnp.int32, (1, PAGE_SIZE), 1)
        sc_n = jnp.where(lane_n <= tok, scores(kin_ref[0]), -jnp.inf)
        keyn_scr[...] = _order_key(sc_n)
        lane_p = lax.broadcasted_iota(jnp.int32, (1, width), 1)
        idx_n = past + lane_n

        def count(pred):
            part = jnp.zeros((t_new, width), jnp.int32)
            for s in range(nj):
                part = part + pred(key_scr[s], s * width + lane_p).astype(jnp.int32)
            return (jnp.sum(part, axis=1, keepdims=True)
                    + jnp.sum(pred(keyn_scr[...], idx_n).astype(jnp.int32), axis=1, keepdims=True))

        def thr_bit(bi, t):
            cand = t + jnp.left_shift(jnp.int32(1), 31 - bi)
            return jnp.where(count(lambda k, idx: k >= cand) >= topk, cand, t)

        thr = lax.fori_loop(0, 32, thr_bit, jnp.full((t_new, 1), INT32_MIN, jnp.int32))
        need = topk - count(lambda k, idx: k > thr)

        def tie_bit(bi, jb):
            cand = jb + jnp.left_shift(jnp.int32(1), idx_bits - 1 - bi)
            before = count(lambda k, idx: (k == thr) & (idx < cand))
            return jnp.where(before <= need - 1, cand, jb)

        tie = lax.fori_loop(0, idx_bits, tie_bit, jnp.zeros((t_new, 1), jnp.int32))

        def bias_of(k, idx):
            return jnp.where((k > thr) | ((k == thr) & (idx <= tie)), 0.0, MASK_BIAS)

        for s in range(nj):
            idx_s = s * width + lax.broadcasted_iota(jnp.int32, (1, width), 1)
            bias_ref[0, :, s * width:(s + 1) * width] = bias_of(key_scr[s], idx_s)
        bias_ref[0, :, past:past + PAGE_SIZE] = jnp.where(lane_n <= tok, bias_of(keyn_scr[...], idx_n), MASK_BIAS)


def _dsa_sample_attn_body(pt_ref, q_ref, bias_ref, biasn_ref, knt_ref, vnt_ref, *rest, pg, t_new):
    k_refs, v_refs = rest[:pg], rest[pg:2 * pg]
    o_ref, m_ref, l_ref, acc_ref = rest[2 * pg:]
    j = pl.program_id(1)
    bf = jnp.bfloat16

    @pl.when(j == 0)
    def _():
        m_ref[...] = jnp.full(m_ref.shape, M_INIT, jnp.float32)
        l_ref[...] = jnp.zeros(l_ref.shape, jnp.float32)
        acc_ref[...] = jnp.zeros(acc_ref.shape, jnp.float32)

    def update(kt, vt, bias):
        s = jnp.dot(q_ref[0], kt.astype(bf), preferred_element_type=jnp.float32)
        s = s + jnp.concatenate([bias] * HC, axis=0)
        m_prev = m_ref[...]
        m_new = jnp.maximum(m_prev, jnp.max(s, axis=1, keepdims=True))
        alpha = jnp.exp(m_prev - m_new)
        p = jnp.exp(s - m_new[:, :1])
        l_ref[...] = alpha * l_ref[...] + jnp.sum(p, axis=1, keepdims=True)
        pv = lax.dot_general(p.astype(bf), vt.astype(bf), (((1,), (1,)), ((), ())),
                             preferred_element_type=jnp.float32)
        acc_ref[...] = acc_ref[...] * jnp.concatenate([alpha] * (WC // LANES), axis=1) + pv
        m_ref[...] = m_new

    update(jnp.concatenate([r[...] for r in k_refs], axis=1),
           jnp.concatenate([r[...] for r in v_refs], axis=1), bias_ref[0])

    @pl.when(j == pl.num_programs(1) - 1)
    def _():
        update(knt_ref[0], vnt_ref[0], biasn_ref[0])
        head_of_lane = lax.broadcasted_iota(jnp.int32, (1, WC), 1) // DHC
        out = jnp.zeros((t_new, WC), jnp.float32)
        for h in range(HC):
            rows = slice(h * t_new, (h + 1) * t_new)
            o_h = acc_ref[rows] / jnp.concatenate([l_ref[rows]] * (WC // LANES), axis=1)
            out = jnp.where(head_of_lane == h, o_h, out)
        o_ref[0] = out


def dsa_sample_pallas(q, k_new, v_new, qi, wi, ki_new, pool_k, pool_v, pool_ki, page_table, layer,
                      *, pg=PAGES_PER_STEP):
    db, t_new, _ = q.shape
    n_pages = page_table.shape[1]
    past = n_pages * PAGE_SIZE
    assert n_pages % pg == 0 and t_new <= PAGE_SIZE
    nj = n_pages // pg
    width = pg * PAGE_SIZE
    total = past + PAGE_SIZE
    topk = min(TOPK_MAX, (past + t_new) // 4)
    bf = jnp.bfloat16
    n_layers, n_pool = pool_k.shape[:2]
    kit_pool = pool_ki.transpose(0, 1, 3, 2)
    kt_pool = pool_k.transpose(0, 1, 3, 4, 2).reshape(n_layers, n_pool, WC, PAGE_SIZE)
    vt_pool = pool_v.transpose(0, 1, 3, 4, 2).reshape(n_layers, n_pool, WC, PAGE_SIZE)
    pad_t = lambda x: jnp.pad(x.swapaxes(1, 2), ((0, 0), (0, 0), (0, PAGE_SIZE - t_new)))
    qi2 = qi.reshape(db, t_new, HI, DI).swapaxes(1, 2).reshape(db, HI * t_new, DI).astype(bf)
    w2 = wi.swapaxes(1, 2).reshape(db, HI * t_new, 1)
    seq_spec = lambda shape: pl.BlockSpec((1,) + shape, lambda b, j, pt: (b, 0, 0))
    bias = pl.pallas_call(
        functools.partial(_dsa_sample_select_body, pg=pg, nj=nj, t_new=t_new, topk=topk,
                          idx_bits=int(total).bit_length()),
        grid_spec=pltpu.PrefetchScalarGridSpec(
            num_scalar_prefetch=1, grid=(db, nj),
            in_specs=[seq_spec((HI * t_new, DI)), seq_spec((HI * t_new, 1)), seq_spec((DI, PAGE_SIZE))]
            + _page_specs((DI, PAGE_SIZE), layer, pg),
            out_specs=seq_spec((t_new, total)),
            scratch_shapes=[pltpu.VMEM((nj, t_new, width), jnp.int32),
                            pltpu.VMEM((t_new, PAGE_SIZE), jnp.int32)]),
        out_shape=jax.ShapeDtypeStruct((db, t_new, total), jnp.float32),
        compiler_params=pltpu.CompilerParams(dimension_semantics=("arbitrary", "arbitrary")),
        name="dsa_sample_select",
    )(page_table, qi2, w2, pad_t(ki_new).astype(bf), *([kit_pool] * pg))

    qh = (q * (DHC ** -0.5)).reshape(db, t_new, HC, DHC).swapaxes(1, 2)
    qbd = (qh[:, :, :, None, :] * jnp.eye(HC, dtype=q.dtype)[None, :, None, :, None])
    qbd = qbd.reshape(db, HC * t_new, WC).astype(bf)
    return pl.pallas_call(
        functools.partial(_dsa_sample_attn_body, pg=pg, t_new=t_new),
        grid_spec=pltpu.PrefetchScalarGridSpec(
            num_scalar_prefetch=1, grid=(db, nj),
            in_specs=[seq_spec((HC * t_new, WC)),
                      pl.BlockSpec((1, t_new, width), lambda b, j, pt: (b, 0, j)),
                      pl.BlockSpec((1, t_new, PAGE_SIZE), lambda b, j, pt: (b, 0, n_pages)),
                      seq_spec((WC, PAGE_SIZE)), seq_spec((WC, PAGE_SIZE))]
            + _page_specs((WC, PAGE_SIZE), layer, pg) + _page_specs((WC, PAGE_SIZE), layer, pg),
            out_specs=seq_spec((t_new, WC)),
            scratch_shapes=[pltpu.VMEM((HC * t_new, LANES), jnp.float32),
                            pltpu.VMEM((HC * t_new, LANES), jnp.float32),
                            pltpu.VMEM((HC * t_new, WC), jnp.float32)]),
        out_shape=jax.ShapeDtypeStruct((db, t_new, WC), jnp.float32),
        compiler_params=pltpu.CompilerParams(dimension_semantics=("arbitrary", "arbitrary")),
        name="dsa_sample_attn",
    )(page_table, qbd, bias, bias, pad_t(k_new).astype(bf), pad_t(v_new).astype(bf),
      *([kt_pool] * pg), *([vt_pool] * pg))


def _diff_sample_body(pt_ref, lam_ref, q_ref, knt_ref, vn_ref, g_ref, *rest, pg, t_new, out_scale):
    k_refs, v_refs = rest[:pg], rest[pg:2 * pg]
    o_ref, m_ref, l_ref, acc_ref = rest[2 * pg:]
    j = pl.program_id(1)
    bf = jnp.bfloat16

    @pl.when(j == 0)
    def _():
        m_ref[...] = jnp.full(m_ref.shape, M_INIT, jnp.float32)
        l_ref[...] = jnp.zeros(l_ref.shape, jnp.float32)
        acc_ref[...] = jnp.zeros(acc_ref.shape, jnp.float32)

    def update(kt, v, bias):
        s = jnp.dot(q_ref[0], kt.astype(bf), preferred_element_type=jnp.float32)
        if bias is not None:
            s = s + bias
        m_prev = m_ref[...]
        m_new = jnp.maximum(m_prev, jnp.max(s, axis=1, keepdims=True))
        alpha = jnp.exp(m_prev - m_new)
        p = jnp.exp(s - m_new[:, :1])
        l_ref[...] = alpha * l_ref[...] + jnp.sum(p, axis=1, keepdims=True)
        pv = jnp.dot(p.astype(bf), v.astype(bf), preferred_element_type=jnp.float32)
        acc_ref[...] = acc_ref[...] * jnp.concatenate([alpha] * HB, axis=1) + pv
        m_ref[...] = m_new

    v_pages = [jnp.concatenate([r[pl.ds(h, PAGE_SIZE, stride=HB), :] for h in range(HB)], axis=1) for r in v_refs]
    update(jnp.concatenate([r[...] for r in k_refs], axis=1), jnp.concatenate(v_pages, axis=0), None)

    @pl.when(j == pl.num_programs(1) - 1)
    def _():
        n_rows = 2 * HB * t_new
        tok = lax.broadcasted_iota(jnp.int32, (n_rows, 1), 0) % t_new
        lane_n = lax.broadcasted_iota(jnp.int32, (1, PAGE_SIZE), 1)
        update(knt_ref[0], vn_ref[0], jnp.where(lane_n <= tok, 0.0, MASK_BIAS))
        lam = lam_ref[0]
        outs = []
        for h in range(HB):
            r1 = slice(2 * h * t_new, (2 * h + 1) * t_new)
            r2 = slice((2 * h + 1) * t_new, (2 * h + 2) * t_new)
            cols = slice(h * DVB, (h + 1) * DVB)
            o = acc_ref[r1, cols] / l_ref[r1] - lam * (acc_ref[r2, cols] / l_ref[r2])
            o = o * lax.rsqrt(jnp.mean(o * o, axis=1, keepdims=True) + DIFF_SUBLN_EPS)
            outs.append(o * g_ref[...] * out_scale)
        o_ref[0] = jnp.concatenate(outs, axis=1)


def diff_sample_pallas(q, k_new, v_new, pool_k, pool_v, page_table, layer, lam, subln_g, out_scale,
                       *, pg=PAGES_PER_STEP):
    db, t_new, wq = q.shape
    n_pages = page_table.shape[1]
    assert n_pages % pg == 0 and t_new <= PAGE_SIZE and DVB == LANES
    nj = n_pages // pg
    bf = jnp.bfloat16
    n_layers, n_pool = pool_k.shape[:2]
    kt_pool = pool_k.transpose(0, 1, 3, 4, 5, 2).reshape(n_layers, n_pool, wq, PAGE_SIZE)
    v_pool = pool_v.reshape(n_layers, n_pool, PAGE_SIZE * HB, DVB)
    n_hc = 2 * HB
    qh = (q * (DHB ** -0.5)).reshape(db, t_new, n_hc, DHB).swapaxes(1, 2)
    qbd = (qh[:, :, :, None, :] * jnp.eye(n_hc, dtype=q.dtype)[None, :, None, :, None])
    qbd = qbd.reshape(db, n_hc * t_new, wq).astype(bf)
    knt = jnp.pad(k_new.swapaxes(1, 2), ((0, 0), (0, 0), (0, PAGE_SIZE - t_new))).astype(bf)
    vn = jnp.pad(v_new, ((0, 0), (0, PAGE_SIZE - t_new), (0, 0))).astype(bf)
    seq_spec = lambda shape: pl.BlockSpec((1,) + shape, lambda b, j, pt: (b, 0, 0))
    return pl.pallas_call(
        functools.partial(_diff_sample_body, pg=pg, t_new=t_new, out_scale=out_scale),
        grid_spec=pltpu.PrefetchScalarGridSpec(
            num_scalar_prefetch=1, grid=(db, nj),
            in_specs=[pl.BlockSpec(memory_space=pltpu.SMEM),
                      seq_spec((n_hc * t_new, wq)), seq_spec((wq, PAGE_SIZE)), seq_spec((PAGE_SIZE, WB)),
                      pl.BlockSpec((1, DVB), lambda b, j, pt: (0, 0))]
            + _page_specs((wq, PAGE_SIZE), layer, pg) + _page_specs((PAGE_SIZE * HB, DVB), layer, pg),
            out_specs=seq_spec((t_new, WB)),
            scratch_shapes=[pltpu.VMEM((n_hc * t_new, LANES), jnp.float32),
                            pltpu.VMEM((n_hc * t_new, LANES), jnp.float32),
                            pltpu.VMEM((n_hc * t_new, WB), jnp.float32)]),
        out_shape=jax.ShapeDtypeStruct((db, t_new, WB), jnp.float32),
        compiler_params=pltpu.CompilerParams(dimension_semantics=("arbitrary", "arbitrary"),
                                             vmem_limit_bytes=VMEM_LIMIT_BYTES),
        name="diff_sample",
    )(page_table, lam.reshape(1).astype(jnp.float32), qbd, knt, vn,
      subln_g.reshape(1, DVB).astype(jnp.float32), *([kt_pool] * pg), *([v_pool] * pg))


RWKV_TBLK = LANES // 2
RWKV_PAIRS = HA // 2


def _rwkv_scan_body(r_ref, w_ref, k_ref, kk_ref, b_ref, vt_ref, e_ref, s0_ref, ot_ref, sout_ref,
                    s_scr, vb_scr, sr_scr, *, bb, nsteps):
    ti = pl.program_id(1)

    @pl.when(ti == 0)
    def _():
        s_scr[...] = s0_ref[...]

    slot = lax.broadcasted_iota(jnp.int32, (1, LANES), 1) & (RWKV_TBLK - 1)
    e = e_ref[...]
    bf = jnp.bfloat16
    units = [(bi, p) for bi in range(bb) for p in range(RWKV_PAIRS)]
    n_u = len(units)

    for u, (bi, p) in enumerate(units):
        vt = vt_ref[bi, p, 0]
        lhs = jnp.concatenate([jnp.where(slot == t, vt, 0.0).astype(bf) for t in range(nsteps)], axis=0)
        vb_scr[u] = jnp.dot(lhs, e, preferred_element_type=jnp.float32).reshape(nsteps, NA, LANES)

    def step(t, carry):
        rows = [[ref[bi, pl.ds(t, 1), :] for ref in (r_ref, w_ref, k_ref, kk_ref, b_ref)] for bi in range(bb)]
        ms = []
        for bi, p in units:
            ms.append(s_scr[bi, p] * rows[bi][3][:, p * LANES:(p + 1) * LANES])
        m_hi = [m.astype(bf) for m in ms]
        m_lo = [(m - h.astype(jnp.float32)).astype(bf) for m, h in zip(ms, m_hi)]
        sk_all = jnp.dot(jnp.concatenate(m_hi + m_lo, axis=0), e, preferred_element_type=jnp.float32)
        for u, (bi, p) in enumerate(units):
            cols = slice(p * LANES, (p + 1) * LANES)
            r_t, w_t, k_t, _, b_t = rows[bi]
            sk = sk_all[u * NA:(u + 1) * NA] + sk_all[(n_u + u) * NA:(n_u + u + 1) * NA]
            s = s_scr[bi, p] * w_t[:, cols] - sk * b_t[:, cols] + vb_scr[u, t] * k_t[:, cols]
            s_scr[bi, p] = s
            sr_scr[u, t] = s * r_t[:, cols]
        return carry

    lax.fori_loop(0, nsteps, step, 0)

    for u, (bi, p) in enumerate(units):
        o_b = jnp.dot(sr_scr[u].reshape(nsteps * NA, LANES).astype(bf), e, preferred_element_type=jnp.float32)
        ot = jnp.zeros((NA, LANES), jnp.float32)
        for t in range(nsteps):
            ot = jnp.where(slot == t, o_b[t * NA:(t + 1) * NA], ot)
        ot_ref[bi, p, 0] = ot

    @pl.when(ti == pl.num_programs(1) - 1)
    def _():
        sout_ref[...] = s_scr[...]


def rwkv_scan_pallas(r, w, k, v, kk, b, s0, *, bb=2):
    nb, t, _ = r.shape
    nsteps = min(RWKV_TBLK, t)
    assert nb % bb == 0 and t % nsteps == 0 and LANES == 2 * NA
    nblk = t // nsteps
    vt = v.reshape(nb, nblk, nsteps, RWKV_PAIRS, 2, NA).transpose(0, 3, 1, 5, 4, 2)
    vt = jnp.pad(vt, ((0, 0),) * 5 + ((0, RWKV_TBLK - nsteps),)).reshape(nb, RWKV_PAIRS, nblk, NA, LANES)
    s0p = s0.reshape(nb, RWKV_PAIRS, 2, NA, NA).transpose(0, 1, 3, 2, 4).reshape(nb, RWKV_PAIRS, NA, LANES)
    head_of_lane = jnp.arange(LANES) // NA
    e = (head_of_lane[:, None] == head_of_lane[None, :]).astype(jnp.bfloat16)
    row_spec = pl.BlockSpec((bb, nsteps, WA), lambda bi, ti: (bi, ti, 0))
    col_spec = pl.BlockSpec((bb, RWKV_PAIRS, 1, NA, LANES), lambda bi, ti: (bi, 0, ti, 0, 0))
    st_spec = pl.BlockSpec((bb, RWKV_PAIRS, NA, LANES), lambda bi, ti: (bi, 0, 0, 0))
    ot, s_fin = pl.pallas_call(
        functools.partial(_rwkv_scan_body, bb=bb, nsteps=nsteps),
        grid=(nb // bb, nblk),
        in_specs=[row_spec, row_spec, row_spec, row_spec, row_spec, col_spec,
                  pl.BlockSpec((LANES, LANES), lambda bi, ti: (0, 0)), st_spec],
        out_specs=[col_spec, st_spec],
        out_shape=[jax.ShapeDtypeStruct((nb, RWKV_PAIRS, nblk, NA, LANES), jnp.float32),
                   jax.ShapeDtypeStruct((nb, RWKV_PAIRS, NA, LANES), jnp.float32)],
        scratch_shapes=[pltpu.VMEM((bb, RWKV_PAIRS, NA, LANES), jnp.float32),
                        pltpu.VMEM((bb * RWKV_PAIRS, nsteps, NA, LANES), jnp.float32),
                        pltpu.VMEM((bb * RWKV_PAIRS, nsteps, NA, LANES), jnp.float32)],
        compiler_params=pltpu.CompilerParams(dimension_semantics=("arbitrary", "arbitrary"),
                                             vmem_limit_bytes=VMEM_LIMIT_BYTES),
        name="rwkv_scan",
    )(r, w, k, kk, b, vt, e, s0p)
    o = ot.reshape(nb, RWKV_PAIRS, nblk, NA, 2, RWKV_TBLK)[..., :nsteps]
    o = o.transpose(0, 2, 5, 1, 4, 3).reshape(nb, t, WA)
    s_fin = s_fin.reshape(nb, RWKV_PAIRS, NA, 2, NA).transpose(0, 1, 3, 2, 4).reshape(nb, HA, NA, NA)
    return o, s_fin


def split_last(z, sizes):
    offs = []
    acc = 0
    for s in sizes[:-1]:
        acc += s
        offs.append(acc)
    return jnp.split(z, offs, axis=-1)


def rmsnorm(x, g, eps=NORM_EPS):
    xf = x.astype(jnp.float32)
    y = xf * lax.rsqrt(jnp.mean(xf * xf, axis=-1, keepdims=True) + eps)
    return (y * g.astype(jnp.float32)).astype(x.dtype)


def to_blocks(z):
    b, s = z.shape[:2]
    return z.reshape(b, s // QBLOCK, QBLOCK, *z.shape[2:]).swapaxes(0, 1)


def from_blocks(z):
    nb, b = z.shape[:2]
    return z.swapaxes(0, 1).reshape(b, nb * QBLOCK, *z.shape[3:])


def rwkv7_mix(pa, prev, s0, mu, w0, w_up, a0, a_up, g_up, k_k, k_a, r_k, ln_w, ln_b):
    B, T, _ = pa.shape
    f = lambda z: z.astype(jnp.float32)
    p = f(pa)
    shifted = jnp.concatenate([f(prev)[:, None], p[:, :-1]], axis=1)
    px = p + (shifted - p) * f(mu)
    r, k, v, wd, ad, gd = split_last(px, [WA, WA, WA, LORA_W, LORA_A, LORA_G])
    w = -jax.nn.softplus(-(f(w0) + jnp.tanh(wd) @ f(w_up))) - 0.5
    decay = jnp.exp(-jnp.exp(w))
    a = jax.nn.sigmoid(f(a0) + ad @ f(a_up))
    g = jax.nn.sigmoid(gd) @ f(g_up)
    hd = lambda z: z.reshape(B, T, HA, NA)
    kk = hd(k * f(k_k))
    kk = kk * lax.rsqrt(jnp.maximum(jnp.sum(kk * kk, -1, keepdims=True), 1e-24))
    k = k * (1.0 + (a - 1.0) * f(k_a))
    kk_flat = kk.reshape(B, T, WA)
    o, s_fin = rwkv_scan_pallas(r, decay, k, v, kk_flat, kk_flat * a, f(s0))
    o = hd(o)
    r, k, v = hd(r), hd(k), hd(v)
    mean = jnp.mean(o, -1, keepdims=True)
    var = jnp.mean(jnp.square(o - mean), -1, keepdims=True)
    o = ((o - mean) * lax.rsqrt(var + RWKV_GN_EPS)).reshape(B, T, WA) * f(ln_w) + f(ln_b)
    bonus = jnp.sum(r * k * f(r_k), -1, keepdims=True) * v
    o = (o + bonus.reshape(B, T, WA)) * g
    return o, s_fin.astype(s0.dtype), pa[:, -1]


def diff_attend(q, k, v, mask, lam):
    s = jnp.einsum('bqhcd,bkhcd->bhcqk', q.astype(jnp.float32), k.astype(jnp.float32)) * (DHB ** -0.5)
    p = jax.nn.softmax(jnp.where(mask, s, -jnp.inf), axis=-1)
    pd = p[:, :, 0] - lam * p[:, :, 1]
    return jnp.einsum('bhqk,bkhd->bqhd', pd, v.astype(jnp.float32))


def diff_prompt(q, k, v, lam):
    S = q.shape[1]
    kpos = jnp.arange(S)

    def block(args):
        q_i, i = args
        qpos = i * QBLOCK + jnp.arange(QBLOCK)
        return diff_attend(q_i, k, v, qpos[:, None] >= kpos[None, :], lam)

    return from_blocks(lax.map(block, (to_blocks(q), jnp.arange(S // QBLOCK))))


def diff_sample(q, k_new, v_new, pool_k, pool_v, page_table, l, lam):
    T = q.shape[1]
    past = page_table.shape[1] * PAGE_SIZE
    mask = jnp.concatenate([jnp.ones((T, past), bool), jnp.tril(jnp.ones((T, T), bool))], axis=1)

    def one(args):
        q_b, kn_b, vn_b, pt_b = args
        kp = pool_k[l, pt_b].reshape(past, HB, 2, DHB)
        vp = pool_v[l, pt_b].reshape(past, HB, DVB)
        kc = jnp.concatenate([kp, kn_b], axis=0)[None]
        vc = jnp.concatenate([vp, vn_b], axis=0)[None]
        return diff_attend(q_b[None], kc, vc, mask, lam)[0]

    return lax.map(one, (q, k_new, v_new, page_table))


def indexer_scores(qi, wi, ki):
    qk = jnp.einsum('bthd,bsd->bths', qi.astype(jnp.float32), ki.astype(jnp.float32))
    return jnp.einsum('bth,bths->bts', wi.astype(jnp.float32), jax.nn.relu(qk))


def sparse_attend(q, k_sel, v_sel, valid):
    s = jnp.einsum('bthd,btkhd->bthk', q.astype(jnp.float32), k_sel.astype(jnp.float32)) * (DHC ** -0.5)
    p = jax.nn.softmax(jnp.where(valid[:, :, None, :], s, -jnp.inf), axis=-1)
    return jnp.einsum('bthk,btkhd->bthd', p, v_sel.astype(jnp.float32))


def dsa_prompt(q, k, v, qi, wi, ki):
    B, S = q.shape[:2]
    topk = min(TOPK_MAX, S // 4)
    kpos = jnp.arange(S)
    b_idx = jnp.arange(B)[:, None, None]

    def block(args):
        q_i, qi_i, wi_i, i = args
        qpos = i * QBLOCK + jnp.arange(QBLOCK)
        sc = jnp.where(qpos[:, None] >= kpos[None, :], indexer_scores(qi_i, wi_i, ki), -jnp.inf)
        _, sel = lax.top_k(sc, topk)
        return sparse_attend(q_i, k[b_idx, sel], v[b_idx, sel], sel <= qpos[None, :, None])

    o = lax.map(block, (to_blocks(q), to_blocks(qi), to_blocks(wi), jnp.arange(S // QBLOCK)))
    return from_blocks(o)


def dsa_sample(q, k_new, v_new, qi, wi, ki_new, pool_k, pool_v, pool_ki, page_table, l):
    DB, T = q.shape[:2]
    past = page_table.shape[1] * PAGE_SIZE
    L = past + T
    topk = min(TOPK_MAX, L // 4)
    ki_all = jnp.concatenate([pool_ki[l, page_table].reshape(DB, past, DI), ki_new], axis=1)
    qpos = past + jnp.arange(T)
    sc = jnp.where(qpos[:, None] >= jnp.arange(L)[None, :], indexer_scores(qi, wi, ki_all), -jnp.inf)
    _, sel = lax.top_k(sc, topk)
    b_idx = jnp.arange(DB)[:, None, None]
    from_past = (sel < past)[..., None, None]
    s_p = jnp.minimum(sel, past - 1)
    phys = page_table[b_idx, s_p // PAGE_SIZE]
    off = s_p % PAGE_SIZE
    s_n = jnp.clip(sel - past, 0, T - 1)
    k_sel = jnp.where(from_past, pool_k[l, phys, off], k_new[b_idx, s_n])
    v_sel = jnp.where(from_past, pool_v[l, phys, off], v_new[b_idx, s_n])
    return sparse_attend(q, k_sel, v_sel, sel <= qpos[None, :, None])


def kernel(x_prompt, x_sample, state_rwkv, state_shift, cache_diff_k, cache_diff_v, cache_dsa_k,
           cache_dsa_v, cache_idx_k, page_table, g_mix, w_in, rwkv_mu, rwkv_w0, rwkv_w_up, rwkv_a0,
           rwkv_a_up, rwkv_g_up, rwkv_k_k, rwkv_k_a, rwkv_r_k, rwkv_ln_w, rwkv_ln_b, diff_lam_q1,
           diff_lam_k1, diff_lam_q2, diff_lam_k2, diff_subln_g, w_br_a, w_br_b, w_br_c, w_out, g_ffn,
           w_ffn_gate, w_ffn_up, w_ffn_down, g_final):
    B = x_prompt.shape[0]

    def project(h, l):
        b, t = h.shape[:2]
        xn = rmsnorm(h, g_mix[l])
        pa, pb, pc, pg = split_last(xn @ w_in[l], [NCA, NCB, NCC, N_BRANCH * D_MODEL])
        qb, kb, vb = split_last(pb, [HB * 2 * DHB, HB * 2 * DHB, WB])
        qc, kc, vc, qi, ki, wi = split_last(pc, [WC, WC, WC, HI * DI, DI, HI])
        diff = (qb.reshape(b, t, HB, 2, DHB), kb.reshape(b, t, HB, 2, DHB), vb.reshape(b, t, HB, DVB))
        dsa = (qc.reshape(b, t, HC, DHC), kc.reshape(b, t, HC, DHC), vc.reshape(b, t, HC, DHC),
               qi.reshape(b, t, HI, DI) * (DI ** -0.5), wi * (HI ** -0.5), ki)
        gates = jax.nn.sigmoid(pg).reshape(b, t, N_BRANCH, D_MODEL)
        return pa, diff, dsa, gates

    def finish(h, oa, ob, oc, gates, l, lam_init):
        b, t = h.shape[:2]
        br_a = oa.astype(h.dtype) @ w_br_a[l]
        br_b = ob.reshape(b, t, WB).astype(h.dtype) @ w_br_b[l]
        br_c = oc.reshape(b, t, WC).astype(h.dtype) @ w_br_c[l]
        merged = gates[:, :, 0] * br_a + gates[:, :, 1] * br_b + gates[:, :, 2] * br_c
        h = h + merged @ w_out[l]
        xn = rmsnorm(h, g_ffn[l])
        return h + (jax.nn.silu(xn @ w_ffn_gate[l]) * (xn @ w_ffn_up[l])) @ w_ffn_down[l]

    names = ('rwkv', 'shift', 'dk', 'dv', 'ck', 'cv', 'ik')
    newp = {n: [] for n in names}
    news = {n: [] for n in names}
    hp, hs = x_prompt, x_sample
    for l in range(DEPTH):
        lam_init = 0.8 - 0.6 * math.exp(-0.3 * l)
        lam = (jnp.exp(jnp.sum(diff_lam_q1[l] * diff_lam_k1[l])) - jnp.exp(jnp.sum(diff_lam_q2[l] * diff_lam_k2[l]))
               + lam_init).astype(jnp.float32)
        rw = (rwkv_mu[l], rwkv_w0[l], rwkv_w_up[l], rwkv_a0[l], rwkv_a_up[l], rwkv_g_up[l], rwkv_k_k[l],
              rwkv_k_a[l], rwkv_r_k[l], rwkv_ln_w[l], rwkv_ln_b[l])

        pa, (qb, kb, vb), (qc, kc, vc, qi, wi, ki), gates = project(hp, l)
        oa, s_a, sh_a = rwkv7_mix(pa, jnp.zeros((B, NCA), pa.dtype),
                                  jnp.zeros((B, HA, NA, NA), state_rwkv.dtype), *rw)
        bp, tp = hp.shape[:2]
        ob = diff_prompt_pallas(qb.reshape(bp, tp, -1), kb.reshape(bp, tp, -1), vb.reshape(bp, tp, -1),
                                lam, diff_subln_g[l], 1.0 - lam_init)
        oc = dsa_prompt_pallas(qc.reshape(bp, tp, -1), kc.reshape(bp, tp, -1), vc.reshape(bp, tp, -1),
                               qi.reshape(bp, tp, -1), wi, ki)
        hp = finish(hp, oa, ob, oc, gates, l, lam_init)
        for n, val in zip(names, (s_a, sh_a, kb, vb, kc, vc, ki)):
            newp[n].append(val)

        pa, (qb, kb, vb), (qc, kc, vc, qi, wi, ki), gates = project(hs, l)
        oa, s_a, sh_a = rwkv7_mix(pa, state_shift[l], state_rwkv[l], *rw)
        bs, ts = hs.shape[:2]
        ob = diff_sample_pallas(qb.reshape(bs, ts, -1), kb.reshape(bs, ts, -1), vb.reshape(bs, ts, -1),
                                cache_diff_k, cache_diff_v, page_table, l, lam, diff_subln_g[l], 1.0 - lam_init)
        oc = dsa_sample_pallas(qc.reshape(bs, ts, -1), kc.reshape(bs, ts, -1), vc.reshape(bs, ts, -1),
                               qi.reshape(bs, ts, -1), wi, ki, cache_dsa_k, cache_dsa_v, cache_idx_k,
                               page_table, l)
        hs = finish(hs, oa, ob, oc, gates, l, lam_init)
        for n, val in zip(names, (s_a, sh_a, kb, vb, kc, vc, ki)):
            news[n].append(val)

    yp = rmsnorm_pallas(hp.reshape(-1, D_MODEL), g_final).reshape(hp.shape)
    ys = rmsnorm_pallas(hs.reshape(-1, D_MODEL), g_final).reshape(hs.shape)
    return (yp, ys,
            jnp.stack(newp['rwkv']), jnp.stack(newp['shift']), jnp.stack(newp['dk']), jnp.stack(newp['dv']),
            jnp.stack(newp['ck']), jnp.stack(newp['cv']), jnp.stack(newp['ik']),
            jnp.stack(news['rwkv']), jnp.stack(news['shift']), jnp.stack(news['dk']), jnp.stack(news['dv']),
            jnp.stack(news['ck']), jnp.stack(news['cv']), jnp.stack(news['ik']))
```

```python
import functools
import math
import jax
import jax.numpy as jnp
from jax import lax
from jax.experimental import pallas as pl
from jax.experimental.pallas import tpu as pltpu

D_MODEL = 1024
DEPTH = 2
PAGE_SIZE = 128

HA = 4
NA = 64
WA = HA * NA
LORA_W = 64
LORA_A = 64
LORA_G = 128
NCA = 3 * WA + LORA_W + LORA_A + LORA_G
RWKV_GN_EPS = 64e-5
HB = 4
DHB = 64
DVB = 2 * DHB
WB = HB * DVB
NCB = 2 * (HB * 2 * DHB) + WB
DIFF_SUBLN_EPS = 1e-5
HC = 4
DHC = 64
WC = HC * DHC
HI = 8
DI = 64
TOPK_MAX = 256
NCC = 3 * WC + HI * DI + DI + HI
N_BRANCH = 3
N_IN = NCA + NCB + NCC + N_BRANCH * D_MODEL
D_FF = -(-8 * D_MODEL // (3 * 256)) * 256
NORM_EPS = 1e-6

VMEM_LIMIT_BYTES = 56 * 1024 * 1024
LANES = 128
MASK_BIAS = -1e30
M_INIT = -0.5e30
INT32_MIN = -(2 ** 31)


def _flash_update(s, v_bf16, m_ref, l_ref, acc_ref, slot, dv):
    m_prev = m_ref[slot]
    m_new = jnp.maximum(m_prev, jnp.max(s, axis=1, keepdims=True))
    alpha = jnp.exp(m_prev - m_new)
    p = jnp.exp(s - m_new[:, :1])
    l_ref[slot] = alpha * l_ref[slot] + jnp.sum(p, axis=1, keepdims=True)
    pv = jnp.dot(p.astype(jnp.bfloat16), v_bf16, preferred_element_type=jnp.float32)
    acc_ref[slot] = acc_ref[slot] * alpha[:, :dv] + pv
    m_ref[slot] = m_new


def _dsa_prompt_body(qi_ref, wi_ref, q_ref, kit_ref, kt_ref, v_ref, o_ref,
                     key_ref, wb_ref, thr_ref, tie_ref, m_ref, l_ref, acc_ref, *, tq, tk, tr, topk, idx_bits):
    i = pl.program_id(1)
    nkc = (i + 1) * (tq // tk)
    qpos = i * tq + lax.broadcasted_iota(jnp.int32, (tq, 1), 0)

    for h in range(HI):
        wb_ref[h] = jnp.broadcast_to(wi_ref[0, :, h:h + 1], (tq, LANES))

    def score_body(c, carry):
        kit = kit_ref[0, c]
        tiles = [jnp.zeros((tq, LANES), jnp.float32)] * (tk // LANES)
        for h in range(HI):
            qk = jnp.dot(qi_ref[0, :, h * DI:(h + 1) * DI], kit, preferred_element_type=jnp.float32)
            w_b = wb_ref[h]
            tiles = [a + w_b * jnp.maximum(qk[:, j * LANES:(j + 1) * LANES], 0.0) for j, a in enumerate(tiles)]
        acc = jnp.concatenate(tiles, axis=1)
        kpos = c * tk + lax.broadcasted_iota(jnp.int32, (1, tk), 1)
        key_ref[c] = _order_key(jnp.where(kpos <= qpos, acc, -jnp.inf))
        return carry

    lax.fori_loop(0, nkc, score_body, 0)

    def rows_body(rt, carry):
        r0 = pl.multiple_of(rt * tr, tr)

        def count(pred):
            def cbody(c, part):
                kc = key_ref[c, pl.ds(r0, tr), :]
                for j in range(tk // LANES):
                    idx = c * tk + j * LANES + lax.broadcasted_iota(jnp.int32, (1, LANES), 1)
                    part = part + pred(kc[:, j * LANES:(j + 1) * LANES], idx).astype(jnp.int32)
                return part
            part = lax.fori_loop(0, nkc, cbody, jnp.zeros((tr, LANES), jnp.int32))
            return jnp.sum(part, axis=1, keepdims=True)

        def thr_bit(bi, t):
            cand = jnp.broadcast_to(t + jnp.left_shift(jnp.int32(1), 31 - bi), (tr, LANES))
            cnt = count(lambda k, idx: k >= cand)
            return jnp.where(cnt >= topk, cand[:, :1], t)

        thr = lax.fori_loop(0, 32, thr_bit, jnp.full((tr, 1), INT32_MIN, jnp.int32))
        thr_b = jnp.broadcast_to(thr, (tr, LANES))
        n_gt = count(lambda k, idx: k > thr_b)
        n_ge = count(lambda k, idx: k >= thr_b)
        need = topk - n_gt

        def tie_search():
            def tie_bit(bi, jbound):
                cand = jnp.broadcast_to(jbound + jnp.left_shift(jnp.int32(1), idx_bits - 1 - bi), (tr, LANES))
                before = count(lambda k, idx: (k == thr_b) & (idx < cand))
                return jnp.where(before <= need - 1, cand[:, :1], jbound)
            return lax.fori_loop(0, idx_bits, tie_bit, jnp.zeros((tr, 1), jnp.int32))

        tie = lax.cond(jnp.max(n_ge) > topk, tie_search,
                       lambda: jnp.full((tr, 1), 2 ** 31 - 1, jnp.int32))
        thr_ref[pl.ds(r0, tr), :] = thr_b
        tie_ref[pl.ds(r0, tr), :] = jnp.broadcast_to(tie, (tr, LANES))
        return carry

    lax.fori_loop(0, tq // tr, rows_body, 0)

    m_ref[...] = jnp.full(m_ref.shape, M_INIT, jnp.float32)
    l_ref[...] = jnp.zeros(l_ref.shape, jnp.float32)
    acc_ref[...] = jnp.zeros(acc_ref.shape, jnp.float32)
    thr_c = thr_ref[:, :1]
    tie_c = tie_ref[:, :1]

    def att_body(c, carry):
        kc = key_ref[c]
        idx = c * tk + lax.broadcasted_iota(jnp.int32, (1, tk), 1)
        sel = (kc > thr_c) | ((kc == thr_c) & (idx <= tie_c))
        bias = jnp.where(sel & (idx <= qpos), 0.0, MASK_BIAS)
        kt = kt_ref[0, c]
        vv = v_ref[0, c]
        for h in range(HC):
            s = jnp.dot(q_ref[0, :, h * DHC:(h + 1) * DHC], kt[h * DHC:(h + 1) * DHC, :],
                        preferred_element_type=jnp.float32) + bias
            _flash_update(s, vv[:, h * DHC:(h + 1) * DHC], m_ref, l_ref, acc_ref, h, DHC)
        return carry

    lax.fori_loop(0, nkc, att_body, 0)
    o_ref[0] = jnp.concatenate([acc_ref[h] / l_ref[h][:, :DHC] for h in range(HC)], axis=1)


def dsa_prompt_pallas(q, k, v, qi, wi, ki, *, tq=256, tk=256, tr=128):
    b, s, _ = q.shape
    topk = min(TOPK_MAX, s // 4)
    assert s % tq == 0 and tq % tk == 0 and tq % tr == 0 and tq >= topk
    nc = s // tk
    bf = jnp.bfloat16
    kit = ki.astype(bf).reshape(b, nc, tk, DI).swapaxes(2, 3)
    kt = k.astype(bf).reshape(b, nc, tk, WC).swapaxes(2, 3)
    vc = v.astype(bf).reshape(b, nc, tk, WC)
    body = functools.partial(_dsa_prompt_body, tq=tq, tk=tk, tr=tr, topk=topk, idx_bits=int(s).bit_length())
    return pl.pallas_call(
        body,
        grid=(b, s // tq),
        in_specs=[
            pl.BlockSpec((1, tq, HI * DI), lambda bi, i: (bi, i, 0)),
            pl.BlockSpec((1, tq, HI), lambda bi, i: (bi, i, 0)),
            pl.BlockSpec((1, tq, WC), lambda bi, i: (bi, i, 0)),
            pl.BlockSpec((1, nc, DI, tk), lambda bi, i: (bi, 0, 0, 0)),
            pl.BlockSpec((1, nc, WC, tk), lambda bi, i: (bi, 0, 0, 0)),
            pl.BlockSpec((1, nc, tk, WC), lambda bi, i: (bi, 0, 0, 0)),
        ],
        out_specs=pl.BlockSpec((1, tq, WC), lambda bi, i: (bi, i, 0)),
        out_shape=jax.ShapeDtypeStruct((b, s, WC), jnp.float32),
        scratch_shapes=[
            pltpu.VMEM((nc, tq, tk), jnp.int32),
            pltpu.VMEM((HI, tq, LANES), jnp.float32),
            pltpu.VMEM((tq, LANES), jnp.int32),
            pltpu.VMEM((tq, LANES), jnp.int32),
            pltpu.VMEM((HC, tq, LANES), jnp.float32),
            pltpu.VMEM((HC, tq, LANES), jnp.float32),
            pltpu.VMEM((HC, tq, DHC), jnp.float32),
        ],
        compiler_params=pltpu.CompilerParams(dimension_semantics=("arbitrary", "arbitrary"),
                                             vmem_limit_bytes=VMEM_LIMIT_BYTES),
        name="dsa_prompt",
    )(qi.astype(bf), wi, q.astype(bf), kit, kt, vc)


def _diff_prompt_body(lam_ref, q_ref, kt_ref, v_ref, g_ref, o_ref, m_ref, l_ref, acc_ref,
                      *, tq, tk, out_scale):
    i = pl.program_id(1)
    ratio = tq // tk
    qpos = i * tq + lax.broadcasted_iota(jnp.int32, (tq, 1), 0)
    m_ref[...] = jnp.full(m_ref.shape, M_INIT, jnp.float32)
    l_ref[...] = jnp.zeros(l_ref.shape, jnp.float32)
    acc_ref[...] = jnp.zeros(acc_ref.shape, jnp.float32)

    def step(c, masked):
        kt = kt_ref[0, c]
        vv = v_ref[0, c]
        if masked:
            kpos = c * tk + lax.broadcasted_iota(jnp.int32, (1, tk), 1)
            bias = jnp.where(kpos <= qpos, 0.0, MASK_BIAS)
        for h in range(HB):
            for comp in range(2):
                o = (h * 2 + comp) * DHB
                s = jnp.dot(q_ref[0, :, o:o + DHB], kt[o:o + DHB, :], preferred_element_type=jnp.float32)
                if masked:
                    s = s + bias
                _flash_update(s, vv[:, h * DVB:(h + 1) * DVB], m_ref, l_ref, acc_ref, h * 2 + comp, DVB)

    def full_body(c, carry):
        step(c, False)
        return carry

    def diag_body(c, carry):
        step(c, True)
        return carry

    lax.fori_loop(0, i * ratio, full_body, 0)
    lax.fori_loop(i * ratio, (i + 1) * ratio, diag_body, 0)

    lam = lam_ref[0]
    outs = []
    for h in range(HB):
        o1 = acc_ref[2 * h] / l_ref[2 * h]
        o2 = acc_ref[2 * h + 1] / l_ref[2 * h + 1]
        o = o1 - lam * o2
        o = o * lax.rsqrt(jnp.mean(o * o, axis=1, keepdims=True) + DIFF_SUBLN_EPS)
        outs.append(o * g_ref[...] * out_scale)
    o_ref[0] = jnp.concatenate(outs, axis=1)


def diff_prompt_pallas(q, k, v, lam, subln_g, out_scale, *, tq=512, tk=512):
    b, s, _ = q.shape
    tq, tk = min(tq, s), min(tk, s)
    assert s % tq == 0 and tq % tk == 0 and DVB == LANES
    nc = s // tk
    bf = jnp.bfloat16
    wq = HB * 2 * DHB
    kt = k.astype(bf).reshape(b, nc, tk, wq).swapaxes(2, 3)
    vc = v.astype(bf).reshape(b, nc, tk, WB)
    body = functools.partial(_diff_prompt_body, tq=tq, tk=tk, out_scale=out_scale)
    return pl.pallas_call(
        body,
        grid=(b, s // tq),
        in_specs=[
            pl.BlockSpec(memory_space=pltpu.SMEM),
            pl.BlockSpec((1, tq, wq), lambda bi, i: (bi, i, 0)),
            pl.BlockSpec((1, nc, wq, tk), lambda bi, i: (bi, 0, 0, 0)),
            pl.BlockSpec((1, nc, tk, WB), lambda bi, i: (bi, 0, 0, 0)),
            pl.BlockSpec((1, DVB), lambda bi, i: (0, 0)),
        ],
        out_specs=pl.BlockSpec((1, tq, WB), lambda bi, i: (bi, i, 0)),
        out_shape=jax.ShapeDtypeStruct((b, s, WB), jnp.float32),
        scratch_shapes=[
            pltpu.VMEM((HB * 2, tq, LANES), jnp.float32),
            pltpu.VMEM((HB * 2, tq, LANES), jnp.float32),
            pltpu.VMEM((HB * 2, tq, DVB), jnp.float32),
        ],
        compiler_params=pltpu.CompilerParams(dimension_semantics=("arbitrary", "arbitrary"),
                                             vmem_limit_bytes=VMEM_LIMIT_BYTES),
        name="diff_prompt",
    )(lam.reshape(1).astype(jnp.float32), q.astype(bf), kt, vc, subln_g.reshape(1, DVB).astype(jnp.float32))


PAGES_PER_STEP = 8


def _order_key(x):
    x = jnp.where(x == 0.0, 0.0, x)
    bits = pltpu.bitcast(x, jnp.int32)
    return bits ^ ((bits >> 31) & 0x7FFFFFFF)


def _page_specs(block, layer, n_rep):
    def spec(g):
        return pl.BlockSpec((None, None) + block,
                            lambda b, j, pt: (layer, pt[b, j * n_rep + g], 0, 0))
    return [spec(g) for g in range(n_rep)]


def _dsa_sample_select_body(pt_ref, qi_ref, w_ref, kin_ref, *rest, pg, nj, t_new, topk, idx_bits):
    page_refs, (bias_ref, key_scr, keyn_scr) = rest[:pg], rest[pg:]
    j = pl.program_id(1)
    width = pg * PAGE_SIZE
    bf = jnp.bfloat16

    def scores(kit):
        qk = jnp.dot(qi_ref[0], kit.astype(bf), preferred_element_type=jnp.float32)
        sc = jnp.maximum(qk, 0.0) * w_ref[0]
        out = sc[0:t_new]
        for h in range(1, HI):
            out = out + sc[h * t_new:(h + 1) * t_new]
        return out

    key_scr[j] = _order_key(scores(jnp.concatenate([r[...] for r in page_refs], axis=1)))

    @pl.when(j == nj - 1)
    def _():
        past = nj * width
        tok = lax.broadcasted_iota(jnp.int32, (t_new, 1), 0)
        lane_n = lax.broadcasted_iota(jnp.int32, (1, PAGE_SIZE), 1)
        sc_n = jnp.where(lane_n <= tok, scores(kin_ref[0]), -jnp.inf)
        keyn_scr[...] = _order_key(sc_n)
        lane_p = lax.broadcasted_iota(jnp.int32, (1, width), 1)
        idx_n = past + lane_n

        def count(pred):
            part = jnp.zeros((t_new, width), jnp.int32)
            for s in range(nj):
                part = part + pred(key_scr[s], s * width + lane_p).astype(jnp.int32)
            return (jnp.sum(part, axis=1, keepdims=True)
                    + jnp.sum(pred(keyn_scr[...], idx_n).astype(jnp.int32), axis=1, keepdims=True))

        def thr_bit(bi, t):
            cand = t + jnp.left_shift(jnp.int32(1), 31 - bi)
            return jnp.where(count(lambda k, idx: k >= cand) >= topk, cand, t)

        thr = lax.fori_loop(0, 32, thr_bit, jnp.full((t_new, 1), INT32_MIN, jnp.int32))
        need = topk - count(lambda k, idx: k > thr)

        def tie_bit(bi, jb):
            cand = jb + jnp.left_shift(jnp.int32(1), idx_bits - 1 - bi)
            before = count(lambda k, idx: (k == thr) & (idx < cand))
            return jnp.where(before <= need - 1, cand, jb)

        tie = lax.fori_loop(0, idx_bits, tie_bit, jnp.zeros((t_new, 1), jnp.int32))

        def bias_of(k, idx):
            return jnp.where((k > thr) | ((k == thr) & (idx <= tie)), 0.0, MASK_BIAS)

        for s in range(nj):
            idx_s = s * width + lax.broadcasted_iota(jnp.int32, (1, width), 1)
            bias_ref[0, :, s * width:(s + 1) * width] = bias_of(key_scr[s], idx_s)
        bias_ref[0, :, past:past + PAGE_SIZE] = jnp.where(lane_n <= tok, bias_of(keyn_scr[...], idx_n), MASK_BIAS)


def _dsa_sample_attn_body(pt_ref, q_ref, bias_ref, biasn_ref, knt_ref, vnt_ref, *rest, pg, t_new):
    k_refs, v_refs = rest[:pg], rest[pg:2 * pg]
    o_ref, m_ref, l_ref, acc_ref = rest[2 * pg:]
    j = pl.program_id(1)
    bf = jnp.bfloat16

    @pl.when(j == 0)
    def _():
        m_ref[...] = jnp.full(m_ref.shape, M_INIT, jnp.float32)
        l_ref[...] = jnp.zeros(l_ref.shape, jnp.float32)
        acc_ref[...] = jnp.zeros(acc_ref.shape, jnp.float32)

    def update(kt, vt, bias):
        s = jnp.dot(q_ref[0], kt.astype(bf), preferred_element_type=jnp.float32)
        s = s + jnp.concatenate([bias] * HC, axis=0)
        m_prev = m_ref[...]
        m_new = jnp.maximum(m_prev, jnp.max(s, axis=1, keepdims=True))
        alpha = jnp.exp(m_prev - m_new)
        p = jnp.exp(s - m_new[:, :1])
        l_ref[...] = alpha * l_ref[...] + jnp.sum(p, axis=1, keepdims=True)
        pv = lax.dot_general(p.astype(bf), vt.astype(bf), (((1,), (1,)), ((), ())),
                             preferred_element_type=jnp.float32)
        acc_ref[...] = acc_ref[...] * jnp.concatenate([alpha] * (WC // LANES), axis=1) + pv
        m_ref[...] = m_new

    update(jnp.concatenate([r[...] for r in k_refs], axis=1),
           jnp.concatenate([r[...] for r in v_refs], axis=1), bias_ref[0])

    @pl.when(j == pl.num_programs(1) - 1)
    def _():
        update(knt_ref[0], vnt_ref[0], biasn_ref[0])
        head_of_lane = lax.broadcasted_iota(jnp.int32, (1, WC), 1) // DHC
        out = jnp.zeros((t_new, WC), jnp.float32)
        for h in range(HC):
            rows = slice(h * t_new, (h + 1) * t_new)
            o_h = acc_ref[rows] / jnp.concatenate([l_ref[rows]] * (WC // LANES), axis=1)
            out = jnp.where(head_of_lane == h, o_h, out)
        o_ref[0] = out


def dsa_sample_pallas(q, k_new, v_new, qi, wi, ki_new, pool_k, pool_v, pool_ki, page_table, layer,
                      *, pg=PAGES_PER_STEP):
    db, t_new, _ = q.shape
    n_pages = page_table.shape[1]
    past = n_pages * PAGE_SIZE
    assert n_pages % pg == 0 and t_new <= PAGE_SIZE
    nj = n_pages // pg
    width = pg * PAGE_SIZE
    total = past + PAGE_SIZE
    topk = min(TOPK_MAX, (past + t_new) // 4)
    bf = jnp.bfloat16
    n_layers, n_pool = pool_k.shape[:2]
    kit_pool = pool_ki.transpose(0, 1, 3, 2)
    kt_pool = pool_k.transpose(0, 1, 3, 4, 2).reshape(n_layers, n_pool, WC, PAGE_SIZE)
    vt_pool = pool_v.transpose(0, 1, 3, 4, 2).reshape(n_layers, n_pool, WC, PAGE_SIZE)
    pad_t = lambda x: jnp.pad(x.swapaxes(1, 2), ((0, 0), (0, 0), (0, PAGE_SIZE - t_new)))
    qi2 = qi.reshape(db, t_new, HI, DI).swapaxes(1, 2).reshape(db, HI * t_new, DI).astype(bf)
    w2 = wi.swapaxes(1, 2).reshape(db, HI * t_new, 1)
    seq_spec = lambda shape: pl.BlockSpec((1,) + shape, lambda b, j, pt: (b, 0, 0))
    bias = pl.pallas_call(
        functools.partial(_dsa_sample_select_body, pg=pg, nj=nj, t_new=t_new, topk=topk,
                          idx_bits=int(total).bit_length()),
        grid_spec=pltpu.PrefetchScalarGridSpec(
            num_scalar_prefetch=1, grid=(db, nj),
            in_specs=[seq_spec((HI * t_new, DI)), seq_spec((HI * t_new, 1)), seq_spec((DI, PAGE_SIZE))]
            + _page_specs((DI, PAGE_SIZE), layer, pg),
            out_specs=seq_spec((t_new, total)),
            scratch_shapes=[pltpu.VMEM((nj, t_new, width), jnp.int32),
                            pltpu.VMEM((t_new, PAGE_SIZE), jnp.int32)]),
        out_shape=jax.ShapeDtypeStruct((db, t_new, total), jnp.float32),
        compiler_params=pltpu.CompilerParams(dimension_semantics=("arbitrary", "arbitrary")),
        name="dsa_sample_select",
    )(page_table, qi2, w2, pad_t(ki_new).astype(bf), *([kit_pool] * pg))

    qh = q.reshape(db, t_new, HC, DHC).swapaxes(1, 2)
    qbd = (qh[:, :, :, None, :] * jnp.eye(HC, dtype=q.dtype)[None, :, None, :, None])
    qbd = qbd.reshape(db, HC * t_new, WC).astype(bf)
    return pl.pallas_call(
        functools.partial(_dsa_sample_attn_body, pg=pg, t_new=t_new),
        grid_spec=pltpu.PrefetchScalarGridSpec(
            num_scalar_prefetch=1, grid=(db, nj),
            in_specs=[seq_spec((HC * t_new, WC)),
                      pl.BlockSpec((1, t_new, width), lambda b, j, pt: (b, 0, j)),
                      pl.BlockSpec((1, t_new, PAGE_SIZE), lambda b, j, pt: (b, 0, n_pages)),
                      seq_spec((WC, PAGE_SIZE)), seq_spec((WC, PAGE_SIZE))]
            + _page_specs((WC, PAGE_SIZE), layer, pg) + _page_specs((WC, PAGE_SIZE), layer, pg),
            out_specs=seq_spec((t_new, WC)),
            scratch_shapes=[pltpu.VMEM((HC * t_new, LANES), jnp.float32),
                            pltpu.VMEM((HC * t_new, LANES), jnp.float32),
                            pltpu.VMEM((HC * t_new, WC), jnp.float32)]),
        out_shape=jax.ShapeDtypeStruct((db, t_new, WC), jnp.float32),
        compiler_params=pltpu.CompilerParams(dimension_semantics=("arbitrary", "arbitrary")),
        name="dsa_sample_attn",
    )(page_table, qbd, bias, bias, pad_t(k_new).astype(bf), pad_t(v_new).astype(bf),
      *([kt_pool] * pg), *([vt_pool] * pg))


def _diff_sample_body(pt_ref, lam_ref, q_ref, knt_ref, vn_ref, g_ref, *rest, pg, t_new, out_scale):
    k_refs, v_refs = rest[:pg], rest[pg:2 * pg]
    o_ref, m_ref, l_ref, acc_ref = rest[2 * pg:]
    j = pl.program_id(1)
    bf = jnp.bfloat16

    @pl.when(j == 0)
    def _():
        m_ref[...] = jnp.full(m_ref.shape, M_INIT, jnp.float32)
        l_ref[...] = jnp.zeros(l_ref.shape, jnp.float32)
        acc_ref[...] = jnp.zeros(acc_ref.shape, jnp.float32)

    def update(kt, v, bias):
        s = jnp.dot(q_ref[0], kt.astype(bf), preferred_element_type=jnp.float32)
        if bias is not None:
            s = s + bias
        m_prev = m_ref[...]
        m_new = jnp.maximum(m_prev, jnp.max(s, axis=1, keepdims=True))
        alpha = jnp.exp(m_prev - m_new)
        p = jnp.exp(s - m_new[:, :1])
        l_ref[...] = alpha * l_ref[...] + jnp.sum(p, axis=1, keepdims=True)
        pv = jnp.dot(p.astype(bf), v.astype(bf), preferred_element_type=jnp.float32)
        acc_ref[...] = acc_ref[...] * jnp.concatenate([alpha] * HB, axis=1) + pv
        m_ref[...] = m_new

    v_pages = [jnp.concatenate([r[pl.ds(h, PAGE_SIZE, stride=HB), :] for h in range(HB)], axis=1) for r in v_refs]
    update(jnp.concatenate([r[...] for r in k_refs], axis=1), jnp.concatenate(v_pages, axis=0), None)

    @pl.when(j == pl.num_programs(1) - 1)
    def _():
        n_rows = 2 * HB * t_new
        tok = lax.broadcasted_iota(jnp.int32, (n_rows, 1), 0) % t_new
        lane_n = lax.broadcasted_iota(jnp.int32, (1, PAGE_SIZE), 1)
        update(knt_ref[0], vn_ref[0], jnp.where(lane_n <= tok, 0.0, MASK_BIAS))
        lam = lam_ref[0]
        outs = []
        for h in range(HB):
            r1 = slice(2 * h * t_new, (2 * h + 1) * t_new)
            r2 = slice((2 * h + 1) * t_new, (2 * h + 2) * t_new)
            cols = slice(h * DVB, (h + 1) * DVB)
            o = acc_ref[r1, cols] / l_ref[r1] - lam * (acc_ref[r2, cols] / l_ref[r2])
            o = o * lax.rsqrt(jnp.mean(o * o, axis=1, keepdims=True) + DIFF_SUBLN_EPS)
            outs.append(o * g_ref[...] * out_scale)
        o_ref[0] = jnp.concatenate(outs, axis=1)


def diff_sample_pallas(q, k_new, v_new, pool_k, pool_v, page_table, layer, lam, subln_g, out_scale,
                       *, pg=PAGES_PER_STEP):
    db, t_new, wq = q.shape
    n_pages = page_table.shape[1]
    assert n_pages % pg == 0 and t_new <= PAGE_SIZE and DVB == LANES
    nj = n_pages // pg
    bf = jnp.bfloat16
    n_layers, n_pool = pool_k.shape[:2]
    kt_pool = pool_k.transpose(0, 1, 3, 4, 5, 2).reshape(n_layers, n_pool, wq, PAGE_SIZE)
    v_pool = pool_v.reshape(n_layers, n_pool, PAGE_SIZE * HB, DVB)
    n_hc = 2 * HB
    qh = q.reshape(db, t_new, n_hc, DHB).swapaxes(1, 2)
    qbd = (qh[:, :, :, None, :] * jnp.eye(n_hc, dtype=q.dtype)[None, :, None, :, None])
    qbd = qbd.reshape(db, n_hc * t_new, wq).astype(bf)
    knt = jnp.pad(k_new.swapaxes(1, 2), ((0, 0), (0, 0), (0, PAGE_SIZE - t_new))).astype(bf)
    vn = jnp.pad(v_new, ((0, 0), (0, PAGE_SIZE - t_new), (0, 0))).astype(bf)
    seq_spec = lambda shape: pl.BlockSpec((1,) + shape, lambda b, j, pt: (b, 0, 0))
    return pl.pallas_call(
        functools.partial(_diff_sample_body, pg=pg, t_new=t_new, out_scale=out_scale),
        grid_spec=pltpu.PrefetchScalarGridSpec(
            num_scalar_prefetch=1, grid=(db, nj),
            in_specs=[pl.BlockSpec(memory_space=pltpu.SMEM),
                      seq_spec((n_hc * t_new, wq)), seq_spec((wq, PAGE_SIZE)), seq_spec((PAGE_SIZE, WB)),
                      pl.BlockSpec((1, DVB), lambda b, j, pt: (0, 0))]
            + _page_specs((wq, PAGE_SIZE), layer, pg) + _page_specs((PAGE_SIZE * HB, DVB), layer, pg),
            out_specs=seq_spec((t_new, WB)),
            scratch_shapes=[pltpu.VMEM((n_hc * t_new, LANES), jnp.float32),
                            pltpu.VMEM((n_hc * t_new, LANES), jnp.float32),
                            pltpu.VMEM((n_hc * t_new, WB), jnp.float32)]),
        out_shape=jax.ShapeDtypeStruct((db, t_new, WB), jnp.float32),
        compiler_params=pltpu.CompilerParams(dimension_semantics=("arbitrary", "arbitrary"),
                                             vmem_limit_bytes=VMEM_LIMIT_BYTES),
        name="diff_sample",
    )(page_table, lam.reshape(1).astype(jnp.float32), qbd, knt, vn,
      subln_g.reshape(1, DVB).astype(jnp.float32), *([kt_pool] * pg), *([v_pool] * pg))


RWKV_TBLK = LANES // 2
RWKV_PAIRS = HA // 2


def _rwkv_scan_body(r_ref, w_ref, k_ref, kk_ref, b_ref, vt_ref, e_ref, s0_ref, ot_ref, sout_ref,
                    s_scr, vb_scr, sr_scr, *, bb, nsteps):
    ti = pl.program_id(1)

    @pl.when(ti == 0)
    def _():
        s_scr[...] = s0_ref[...]

    slot = lax.broadcasted_iota(jnp.int32, (1, LANES), 1) & (RWKV_TBLK - 1)
    e = e_ref[...]
    bf = jnp.bfloat16
    units = [(bi, p) for bi in range(bb) for p in range(RWKV_PAIRS)]
    n_u = len(units)

    for u, (bi, p) in enumerate(units):
        vt = vt_ref[bi, p, 0]
        lhs = jnp.concatenate([jnp.where(slot == t, vt, 0.0).astype(bf) for t in range(nsteps)], axis=0)
        vb_scr[u] = jnp.dot(lhs, e, preferred_element_type=jnp.float32).reshape(nsteps, NA, LANES)

    def step(t, carry):
        rows = [[ref[bi, pl.ds(t, 1), :] for ref in (r_ref, w_ref, k_ref, kk_ref, b_ref)] for bi in range(bb)]
        ms = []
        for bi, p in units:
            ms.append(s_scr[bi, p] * rows[bi][3][:, p * LANES:(p + 1) * LANES])
        m_hi = [m.astype(bf) for m in ms]
        m_lo = [(m - h.astype(jnp.float32)).astype(bf) for m, h in zip(ms, m_hi)]
        sk_all = jnp.dot(jnp.concatenate(m_hi + m_lo, axis=0), e, preferred_element_type=jnp.float32)
        for u, (bi, p) in enumerate(units):
            cols = slice(p * LANES, (p + 1) * LANES)
            r_t, w_t, k_t, _, b_t = rows[bi]
            sk = sk_all[u * NA:(u + 1) * NA] + sk_all[(n_u + u) * NA:(n_u + u + 1) * NA]
            s = s_scr[bi, p] * w_t[:, cols] - sk * b_t[:, cols] + vb_scr[u, t] * k_t[:, cols]
            s_scr[bi, p] = s
            sr_scr[u, t] = s * r_t[:, cols]
        return carry

    lax.fori_loop(0, nsteps, step, 0)

    for u, (bi, p) in enumerate(units):
        o_b = jnp.dot(sr_scr[u].reshape(nsteps * NA, LANES).astype(bf), e, preferred_element_type=jnp.float32)
        ot = jnp.zeros((NA, LANES), jnp.float32)
        for t in range(nsteps):
            ot = jnp.where(slot == t, o_b[t * NA:(t + 1) * NA], ot)
        ot_ref[bi, p, 0] = ot

    @pl.when(ti == pl.num_programs(1) - 1)
    def _():
        sout_ref[...] = s_scr[...]


def rwkv_scan_pallas(r, w, k, v, kk, b, s0, *, bb=2):
    nb, t, _ = r.shape
    nsteps = min(RWKV_TBLK, t)
    assert nb % bb == 0 and t % nsteps == 0 and LANES == 2 * NA
    nblk = t // nsteps
    vt = v.reshape(nb, nblk, nsteps, RWKV_PAIRS, 2, NA).transpose(0, 3, 1, 5, 4, 2)
    vt = jnp.pad(vt, ((0, 0),) * 5 + ((0, RWKV_TBLK - nsteps),)).reshape(nb, RWKV_PAIRS, nblk, NA, LANES)
    s0p = s0.reshape(nb, RWKV_PAIRS, 2, NA, NA).transpose(0, 1, 3, 2, 4).reshape(nb, RWKV_PAIRS, NA, LANES)
    head_of_lane = jnp.arange(LANES) // NA
    e = (head_of_lane[:, None] == head_of_lane[None, :]).astype(jnp.bfloat16)
    row_spec = pl.BlockSpec((bb, nsteps, WA), lambda bi, ti: (bi, ti, 0))
    col_spec = pl.BlockSpec((bb, RWKV_PAIRS, 1, NA, LANES), lambda bi, ti: (bi, 0, ti, 0, 0))
    st_spec = pl.BlockSpec((bb, RWKV_PAIRS, NA, LANES), lambda bi, ti: (bi, 0, 0, 0))
    ot, s_fin = pl.pallas_call(
        functools.partial(_rwkv_scan_body, bb=bb, nsteps=nsteps),
        grid=(nb // bb, nblk),
        in_specs=[row_spec, row_spec, row_spec, row_spec, row_spec, col_spec,
                  pl.BlockSpec((LANES, LANES), lambda bi, ti: (0, 0)), st_spec],
        out_specs=[col_spec, st_spec],
        out_shape=[jax.ShapeDtypeStruct((nb, RWKV_PAIRS, nblk, NA, LANES), jnp.float32),
                   jax.ShapeDtypeStruct((nb, RWKV_PAIRS, NA, LANES), jnp.float32)],
        scratch_shapes=[pltpu.VMEM((bb, RWKV_PAIRS, NA, LANES), jnp.float32),
                        pltpu.VMEM((bb * RWKV_PAIRS, nsteps, NA, LANES), jnp.float32),
                        pltpu.VMEM((bb * RWKV_PAIRS, nsteps, NA, LANES), jnp.float32)],
        compiler_params=pltpu.CompilerParams(dimension_semantics=("arbitrary", "arbitrary"),
                                             vmem_limit_bytes=VMEM_LIMIT_BYTES),
        name="rwkv_scan",
    )(r, w, k, kk, b, vt, e, s0p)
    o = ot.reshape(nb, RWKV_PAIRS, nblk, NA, 2, RWKV_TBLK)[..., :nsteps]
    o = o.transpose(0, 2, 5, 1, 4, 3).reshape(nb, t, WA)
    s_fin = s_fin.reshape(nb, RWKV_PAIRS, NA, 2, NA).transpose(0, 1, 3, 2, 4).reshape(nb, HA, NA, NA)
    return o, s_fin


ROW_TILE = 256
N_MAIN = NCA + NCB + NCC
N_MAIN_PAD = -(-N_MAIN // LANES) * LANES
PROJ_SEGMENTS = (
    ("pa", 0, NCA, jnp.float32, 1.0),
    ("qb", NCA, HB * 2 * DHB, jnp.bfloat16, DHB ** -0.5),
    ("kb", NCA + HB * 2 * DHB, HB * 2 * DHB, jnp.float32, 1.0),
    ("vb", NCA + 2 * HB * 2 * DHB, WB, jnp.float32, 1.0),
    ("qc", NCA + NCB, WC, jnp.bfloat16, DHC ** -0.5),
    ("kc", NCA + NCB + WC, WC, jnp.float32, 1.0),
    ("vc", NCA + NCB + 2 * WC, WC, jnp.float32, 1.0),
    ("qi", NCA + NCB + 3 * WC, HI * DI, jnp.bfloat16, DI ** -0.5),
)
KI_START = NCA + NCB + 3 * WC + HI * DI


def _proj_body(x_ref, g_ref, wm_ref, wg_ref, *out_refs):
    seg_refs, (ki_ref, wi_ref, gate_ref) = out_refs[:len(PROJ_SEGMENTS)], out_refs[len(PROJ_SEGMENTS):]
    x = x_ref[...]
    xn = (x * lax.rsqrt(jnp.mean(x * x, axis=-1, keepdims=True) + NORM_EPS) * g_ref[...]).astype(jnp.bfloat16)
    for (name, start, width, dtype, scale), ref in zip(PROJ_SEGMENTS, seg_refs):
        y = jnp.dot(xn, wm_ref[:, start:start + width], preferred_element_type=jnp.float32)
        ref[...] = (y * scale if scale != 1.0 else y).astype(dtype)
    tail = jnp.dot(xn, wm_ref[:, KI_START:KI_START + LANES], preferred_element_type=jnp.float32)
    ki_ref[...] = tail[:, :DI]
    wi_ref[...] = tail[:, DI:DI + HI] * (HI ** -0.5)
    for c in range(N_BRANCH):
        y = jnp.dot(xn, wg_ref[:, c * D_MODEL:(c + 1) * D_MODEL], preferred_element_type=jnp.float32)
        gate_ref[:, c * D_MODEL:(c + 1) * D_MODEL] = jax.nn.sigmoid(y)


def proj_pallas(x2d, g, w_in_l):
    t, d = x2d.shape
    tm = min(ROW_TILE, t)
    assert t % tm == 0 and KI_START % LANES == 0 and KI_START + DI + HI == N_MAIN
    bf = jnp.bfloat16
    w_main = jnp.pad(w_in_l[:, :N_MAIN], ((0, 0), (0, N_MAIN_PAD - N_MAIN))).astype(bf)
    w_gate = w_in_l[:, N_MAIN:].astype(bf)
    row = lambda width: pl.BlockSpec((tm, width), lambda i: (i, 0))
    full = lambda a: pl.BlockSpec(a.shape, lambda i: (0, 0))
    widths = [s[2] for s in PROJ_SEGMENTS] + [DI, HI, N_BRANCH * D_MODEL]
    dtypes = [s[3] for s in PROJ_SEGMENTS] + [jnp.float32] * 3
    outs = pl.pallas_call(
        _proj_body,
        grid=(t // tm,),
        in_specs=[row(d), pl.BlockSpec((1, d), lambda i: (0, 0)), full(w_main), full(w_gate)],
        out_specs=[row(w) for w in widths],
        out_shape=[jax.ShapeDtypeStruct((t, w), dt) for w, dt in zip(widths, dtypes)],
        compiler_params=pltpu.CompilerParams(dimension_semantics=("arbitrary",),
                                             vmem_limit_bytes=VMEM_LIMIT_BYTES),
        name="in_proj",
    )(x2d, g.reshape(1, d), w_main, w_gate)
    names = [s[0] for s in PROJ_SEGMENTS] + ["ki", "wi", "gates"]
    return dict(zip(names, outs))


def _head_sum(x, e):
    x_hi = x.astype(jnp.bfloat16)
    x_lo = (x - x_hi.astype(jnp.float32)).astype(jnp.bfloat16)
    return (jnp.dot(x_hi, e, preferred_element_type=jnp.float32)
            + jnp.dot(x_lo, e, preferred_element_type=jnp.float32))


def _head_ones():
    head_of_lane = jnp.arange(WA) // NA
    return (head_of_lane[:, None] == head_of_lane[None, :]).astype(jnp.bfloat16)


def _rwkv_pre_body(p_ref, ps_ref, mu_ref, w0_ref, wup_ref, a0_ref, aup_ref, gup_ref, kk_ref, ka_ref, rk_ref,
                   e_ref, r_o, w_o, k_o, v_o, kk_o, b_o, g_o, bonus_o):
    bf = jnp.bfloat16
    e = e_ref[...]
    p = p_ref[...]
    px = p + (ps_ref[...] - p) * mu_ref[...]
    r, k, v = px[:, 0:WA], px[:, WA:2 * WA], px[:, 2 * WA:3 * WA]
    o = 3 * WA
    wd, ad, gd = px[:, o:o + LORA_W], px[:, o + LORA_W:o + LORA_W + LORA_A], px[:, o + LORA_W + LORA_A:]
    dot = lambda x, w_ref: jnp.dot(x.astype(bf), w_ref[...], preferred_element_type=jnp.float32)
    z = -(w0_ref[...] + dot(jnp.tanh(wd), wup_ref))
    softplus = jnp.maximum(z, 0.0) + jnp.log(1.0 + jnp.exp(-jnp.abs(z)))
    decay = jnp.exp(-jnp.exp(-softplus - 0.5))
    a = jax.nn.sigmoid(a0_ref[...] + dot(ad, aup_ref))
    kk = k * kk_ref[...]
    kk = kk * lax.rsqrt(jnp.maximum(_head_sum(kk * kk, e), 1e-24))
    k = k * (1.0 + (a - 1.0) * ka_ref[...])
    r_o[...] = r
    w_o[...] = decay
    k_o[...] = k
    v_o[...] = v
    kk_o[...] = kk
    b_o[...] = kk * a
    g_o[...] = dot(jax.nn.sigmoid(gd), gup_ref)
    bonus_o[...] = _head_sum(r * k * rk_ref[...], e) * v


def rwkv_pre_pallas(pa2d, ps2d, mu, w0, w_up, a0, a_up, g_up, k_k, k_a, r_k):
    t, _ = pa2d.shape
    tm = min(ROW_TILE, t)
    bf = jnp.bfloat16
    vecs = [x.reshape(1, -1).astype(jnp.float32) for x in (mu, w0, a0, k_k, k_a, r_k)]
    mats = [x.astype(bf) for x in (w_up, a_up, g_up)]
    row = lambda width: pl.BlockSpec((tm, width), lambda i: (i, 0))
    full = lambda a: pl.BlockSpec(a.shape, lambda i: (0, 0))
    args = [pa2d, ps2d, vecs[0], vecs[1], mats[0], vecs[2], mats[1], mats[2], vecs[3], vecs[4], vecs[5], _head_ones()]
    return pl.pallas_call(
        _rwkv_pre_body,
        grid=(t // tm,),
        in_specs=[row(NCA), row(NCA)] + [full(a) for a in args[2:]],
        out_specs=[row(WA)] * 8,
        out_shape=[jax.ShapeDtypeStruct((t, WA), jnp.float32)] * 8,
        compiler_params=pltpu.CompilerParams(dimension_semantics=("arbitrary",)),
        name="rwkv_pre",
    )(*args)


def _merge_body(h_ref, o_ref, bonus_ref, g_ref, lnw_ref, lnb_ref, e_ref, ob_ref, oc_ref, gate_ref,
                wa_ref, wb_ref, wc_ref, wo_ref, out_ref):
    bf = jnp.bfloat16
    e = e_ref[...]
    o = o_ref[...]
    c = o - _head_sum(o, e) * (1.0 / NA)
    var = _head_sum(c * c, e) * (1.0 / NA)
    oa = (c * lax.rsqrt(var + RWKV_GN_EPS) * lnw_ref[...] + lnb_ref[...] + bonus_ref[...]) * g_ref[...]
    merged = None
    for ci, (x, w_ref) in enumerate(((oa, wa_ref), (ob_ref[...], wb_ref), (oc_ref[...], wc_ref))):
        br = jnp.dot(x.astype(bf), w_ref[...], preferred_element_type=jnp.float32)
        term = gate_ref[:, ci * D_MODEL:(ci + 1) * D_MODEL] * br
        merged = term if merged is None else merged + term
    out_ref[...] = h_ref[...] + jnp.dot(merged.astype(bf), wo_ref[...], preferred_element_type=jnp.float32)


def merge_pallas(h2d, o_rwkv, bonus, g_rwkv, ln_w, ln_b, ob, oc, gates, w_a, w_b, w_c, w_o):
    t, d = h2d.shape
    tm = min(ROW_TILE, t)
    bf = jnp.bfloat16
    ws = [w.astype(bf) for w in (w_a, w_b, w_c, w_o)]
    row = lambda a: pl.BlockSpec((tm, a.shape[1]), lambda i: (i, 0))
    full = lambda a: pl.BlockSpec(a.shape, lambda i: (0, 0))
    lnw, lnb, e = ln_w.reshape(1, WA), ln_b.reshape(1, WA), _head_ones()
    return pl.pallas_call(
        _merge_body,
        grid=(t // tm,),
        in_specs=[row(a) for a in (h2d, o_rwkv, bonus, g_rwkv)] + [full(a) for a in (lnw, lnb, e)]
        + [row(a) for a in (ob, oc, gates)] + [full(w) for w in ws],
        out_specs=row(h2d),
        out_shape=jax.ShapeDtypeStruct((t, d), jnp.float32),
        compiler_params=pltpu.CompilerParams(dimension_semantics=("arbitrary",),
                                             vmem_limit_bytes=VMEM_LIMIT_BYTES),
        name="branch_merge",
    )(h2d, o_rwkv, bonus, g_rwkv, lnw, lnb, e, ob, oc, gates, *ws)


FFN_CHUNK = 256


def _ffn_body(h_ref, g_ref, wg_ref, wu_ref, wd_ref, gf_ref, o_ref, acc_ref, *, final_norm):
    bf = jnp.bfloat16
    h = h_ref[...]
    xn = (h * lax.rsqrt(jnp.mean(h * h, axis=-1, keepdims=True) + NORM_EPS) * g_ref[...]).astype(bf)
    acc_ref[...] = h
    for f in range(0, D_FF, FFN_CHUNK):
        gate = jnp.dot(xn, wg_ref[:, f:f + FFN_CHUNK], preferred_element_type=jnp.float32)
        up = jnp.dot(xn, wu_ref[:, f:f + FFN_CHUNK], preferred_element_type=jnp.float32)
        act = (gate * jax.nn.sigmoid(gate) * up).astype(bf)
        acc_ref[...] += jnp.dot(act, wd_ref[f:f + FFN_CHUNK, :], preferred_element_type=jnp.float32)
    y = acc_ref[...]
    if final_norm:
        y = y * lax.rsqrt(jnp.mean(y * y, axis=-1, keepdims=True) + NORM_EPS) * gf_ref[...]
    o_ref[...] = y


def ffn_pallas(h2d, g, w_gate, w_up, w_down, g_final, *, final_norm):
    t, d = h2d.shape
    tm = min(ROW_TILE, t)
    assert D_FF % FFN_CHUNK == 0
    bf = jnp.bfloat16
    ws = [w.astype(bf) for w in (w_gate, w_up, w_down)]
    row = pl.BlockSpec((tm, d), lambda i: (i, 0))
    vec = pl.BlockSpec((1, d), lambda i: (0, 0))
    full = lambda a: pl.BlockSpec(a.shape, lambda i: (0, 0))
    return pl.pallas_call(
        functools.partial(_ffn_body, final_norm=final_norm),
        grid=(t // tm,),
        in_specs=[row, vec] + [full(w) for w in ws] + [vec],
        out_specs=row,
        out_shape=jax.ShapeDtypeStruct((t, d), jnp.float32),
        scratch_shapes=[pltpu.VMEM((tm, d), jnp.float32)],
        compiler_params=pltpu.CompilerParams(dimension_semantics=("arbitrary",),
                                             vmem_limit_bytes=VMEM_LIMIT_BYTES),
        name="ffn",
    )(h2d, g.reshape(1, d), *ws, g_final.reshape(1, d))


def _layer_group(h, shift_prev, state0, layer, attend, p):
    b, t, d = h.shape
    n = b * t
    pr = proj_pallas(h.reshape(n, d), p["g_mix"][layer], p["w_in"][layer])
    pa = pr["pa"].reshape(b, t, NCA)
    pa_shift = jnp.concatenate([shift_prev[:, None], pa[:, :-1]], axis=1)
    r, decay, k, v, kk, kk_a, g_rwkv, bonus = rwkv_pre_pallas(
        pr["pa"], pa_shift.reshape(n, NCA), p["rwkv_mu"][layer], p["rwkv_w0"][layer], p["rwkv_w_up"][layer],
        p["rwkv_a0"][layer], p["rwkv_a_up"][layer], p["rwkv_g_up"][layer], p["rwkv_k_k"][layer],
        p["rwkv_k_a"][layer], p["rwkv_r_k"][layer])
    seq = lambda x: x.reshape(b, t, WA)
    o_rwkv, s_fin = rwkv_scan_pallas(seq(r), seq(decay), seq(k), seq(v), seq(kk), seq(kk_a), state0)
    ob, oc = attend(pr)
    h2 = merge_pallas(h.reshape(n, d), o_rwkv.reshape(n, WA), bonus, g_rwkv, p["rwkv_ln_w"][layer],
                      p["rwkv_ln_b"][layer], ob, oc, pr["gates"], p["w_br_a"][layer], p["w_br_b"][layer],
                      p["w_br_c"][layer], p["w_out"][layer])
    h3 = ffn_pallas(h2, p["g_ffn"][layer], p["w_ffn_gate"][layer], p["w_ffn_up"][layer], p["w_ffn_down"][layer],
                    p["g_final"], final_norm=(layer == DEPTH - 1))
    caches = (s_fin, pa[:, -1], pr["kb"].reshape(b, t, HB, 2, DHB), pr["vb"].reshape(b, t, HB, DVB),
              pr["kc"].reshape(b, t, HC, DHC), pr["vc"].reshape(b, t, HC, DHC), pr["ki"].reshape(b, t, DI))
    return h3.reshape(b, t, d), caches


def kernel(x_prompt, x_sample, state_rwkv, state_shift, cache_diff_k, cache_diff_v, cache_dsa_k,
           cache_dsa_v, cache_idx_k, page_table, g_mix, w_in, rwkv_mu, rwkv_w0, rwkv_w_up, rwkv_a0,
           rwkv_a_up, rwkv_g_up, rwkv_k_k, rwkv_k_a, rwkv_r_k, rwkv_ln_w, rwkv_ln_b, diff_lam_q1,
           diff_lam_k1, diff_lam_q2, diff_lam_k2, diff_subln_g, w_br_a, w_br_b, w_br_c, w_out, g_ffn,
           w_ffn_gate, w_ffn_up, w_ffn_down, g_final):
    p = dict(g_mix=g_mix, w_in=w_in, rwkv_mu=rwkv_mu, rwkv_w0=rwkv_w0, rwkv_w_up=rwkv_w_up, rwkv_a0=rwkv_a0,
             rwkv_a_up=rwkv_a_up, rwkv_g_up=rwkv_g_up, rwkv_k_k=rwkv_k_k, rwkv_k_a=rwkv_k_a, rwkv_r_k=rwkv_r_k,
             rwkv_ln_w=rwkv_ln_w, rwkv_ln_b=rwkv_ln_b, w_br_a=w_br_a, w_br_b=w_br_b, w_br_c=w_br_c, w_out=w_out,
             g_ffn=g_ffn, w_ffn_gate=w_ffn_gate, w_ffn_up=w_ffn_up, w_ffn_down=w_ffn_down, g_final=g_final)
    bp, tp = x_prompt.shape[:2]
    bs, ts = x_sample.shape[:2]
    newp, news = [], []
    hp, hs = x_prompt, x_sample
    for l in range(DEPTH):
        lam_init = 0.8 - 0.6 * math.exp(-0.3 * l)
        lam = (jnp.exp(jnp.sum(diff_lam_q1[l] * diff_lam_k1[l])) - jnp.exp(jnp.sum(diff_lam_q2[l] * diff_lam_k2[l]))
               + lam_init).astype(jnp.float32)

        def attend_prompt(pr):
            seq = lambda x: x.reshape(bp, tp, -1)
            ob = diff_prompt_pallas(seq(pr["qb"]), seq(pr["kb"]), seq(pr["vb"]), lam, diff_subln_g[l], 1.0 - lam_init)
            oc = dsa_prompt_pallas(seq(pr["qc"]), seq(pr["kc"]), seq(pr["vc"]), seq(pr["qi"]), seq(pr["wi"]),
                                   seq(pr["ki"]))
            return ob.reshape(bp * tp, WB), oc.reshape(bp * tp, WC)

        def attend_sample(pr):
            seq = lambda x: x.reshape(bs, ts, -1)
            ob = diff_sample_pallas(seq(pr["qb"]), seq(pr["kb"]), seq(pr["vb"]), cache_diff_k, cache_diff_v,
                                    page_table, l, lam, diff_subln_g[l], 1.0 - lam_init)
            oc = dsa_sample_pallas(seq(pr["qc"]), seq(pr["kc"]), seq(pr["vc"]), seq(pr["qi"]), seq(pr["wi"]),
                                   seq(pr["ki"]), cache_dsa_k, cache_dsa_v, cache_idx_k, page_table, l)
            return ob.reshape(bs * ts, WB), oc.reshape(bs * ts, WC)

        hp, cp = _layer_group(hp, jnp.zeros((bp, NCA), jnp.float32), jnp.zeros((bp, HA, NA, NA), jnp.float32),
                              l, attend_prompt, p)
        newp.append(cp)
        hs, cs = _layer_group(hs, state_shift[l], state_rwkv[l], l, attend_sample, p)
        news.append(cs)

    stack = lambda group: tuple(jnp.stack([layer[i] for layer in group]) for i in range(7))
    return (hp, hs) + stack(newp) + stack(news)
```

```python
import functools
import math
import jax
import jax.numpy as jnp
from jax import lax
from jax.experimental import pallas as pl
from jax.experimental.pallas import tpu as pltpu

D_MODEL = 1024
DEPTH = 2
PAGE_SIZE = 128

HA = 4
NA = 64
WA = HA * NA
LORA_W = 64
LORA_A = 64
LORA_G = 128
NCA = 3 * WA + LORA_W + LORA_A + LORA_G
RWKV_GN_EPS = 64e-5
HB = 4
DHB = 64
DVB = 2 * DHB
WB = HB * DVB
NCB = 2 * (HB * 2 * DHB) + WB
DIFF_SUBLN_EPS = 1e-5
HC = 4
DHC = 64
WC = HC * DHC
HI = 8
DI = 64
TOPK_MAX = 256
NCC = 3 * WC + HI * DI + DI + HI
N_BRANCH = 3
N_IN = NCA + NCB + NCC + N_BRANCH * D_MODEL
D_FF = -(-8 * D_MODEL // (3 * 256)) * 256
NORM_EPS = 1e-6

VMEM_LIMIT_BYTES = 56 * 1024 * 1024
LANES = 128
MASK_BIAS = -1e30
M_INIT = -0.5e30
INT32_MIN = -(2 ** 31)


def _flash_update(s, v_bf16, m_ref, l_ref, acc_ref, slot, dv):
    m_prev = m_ref[slot]
    m_new = jnp.maximum(m_prev, jnp.max(s, axis=1, keepdims=True))
    alpha = jnp.exp(m_prev - m_new)
    p = jnp.exp(s - m_new[:, :1])
    l_ref[slot] = alpha * l_ref[slot] + jnp.sum(p, axis=1, keepdims=True)
    pv = jnp.dot(p.astype(jnp.bfloat16), v_bf16, preferred_element_type=jnp.float32)
    acc_ref[slot] = acc_ref[slot] * alpha[:, :dv] + pv
    m_ref[slot] = m_new


def _dsa_prompt_body(qi_ref, wi_ref, q_ref, kit_ref, kt_ref, v_ref, o_ref,
                     key_ref, keyt_ref, wb_ref, thr_ref, tie_ref, m_ref, l_ref, acc_ref, *, tq, tk, att_group, topk,
                     idx_bits):
    i = pl.program_id(1)
    nkc = (i + 1) * (tq // tk)
    qpos = i * tq + lax.broadcasted_iota(jnp.int32, (tq, 1), 0)

    for h in range(HI):
        wb_ref[h] = jnp.broadcast_to(wi_ref[0, :, h:h + 1], (tq, LANES))

    def score_body(c, carry):
        kit = kit_ref[0, c]
        tiles = [jnp.zeros((tq, LANES), jnp.float32)] * (tk // LANES)
        for h in range(HI):
            qk = jnp.dot(qi_ref[0, :, h * DI:(h + 1) * DI], kit, preferred_element_type=jnp.float32)
            w_b = wb_ref[h]
            tiles = [a + w_b * jnp.maximum(qk[:, j * LANES:(j + 1) * LANES], 0.0) for j, a in enumerate(tiles)]
        acc = jnp.concatenate(tiles, axis=1)
        kpos = c * tk + lax.broadcasted_iota(jnp.int32, (1, tk), 1)
        key = _order_key(jnp.where(kpos <= qpos, acc, -jnp.inf))
        key_ref[c] = key
        keyt_ref[c] = jnp.transpose(key)
        return carry

    lax.fori_loop(0, nkc, score_body, 0)

    sub = 8

    def count(pred):
        def cbody(c, part):
            kt_c = keyt_ref[c]
            idx = c * tk + lax.broadcasted_iota(jnp.int32, (tk, 1), 0)
            hit = pred(kt_c, idx).astype(jnp.int32)
            for g in range(tk // sub):
                part = part + hit[g * sub:(g + 1) * sub]
            return part
        part = lax.fori_loop(0, nkc, cbody, jnp.zeros((sub, tq), jnp.int32))
        return jnp.sum(part, axis=0, keepdims=True)

    def thr_bit(bi, t):
        cand = t + jnp.left_shift(jnp.int32(1), 31 - bi)
        return jnp.where(count(lambda k, idx: k >= cand) >= topk, cand, t)

    thr = lax.fori_loop(0, 32, thr_bit, jnp.full((1, tq), INT32_MIN, jnp.int32))
    n_gt = count(lambda k, idx: k > thr)
    n_ge = count(lambda k, idx: k >= thr)
    need = topk - n_gt

    def tie_search():
        def tie_bit(bi, jbound):
            cand = jbound + jnp.left_shift(jnp.int32(1), idx_bits - 1 - bi)
            before = count(lambda k, idx: (k == thr) & (idx < cand))
            return jnp.where(before <= need - 1, cand, jbound)
        return lax.fori_loop(0, idx_bits, tie_bit, jnp.zeros((1, tq), jnp.int32))

    tie = lax.cond(jnp.max(n_ge) > topk, tie_search, lambda: jnp.full((1, tq), 2 ** 31 - 1, jnp.int32))
    thr_ref[...] = jnp.transpose(jnp.broadcast_to(thr, (LANES, tq)))
    tie_ref[...] = jnp.transpose(jnp.broadcast_to(tie, (LANES, tq)))

    m_ref[...] = jnp.full(m_ref.shape, M_INIT, jnp.float32)
    l_ref[...] = jnp.zeros(l_ref.shape, jnp.float32)
    acc_ref[...] = jnp.zeros(acc_ref.shape, jnp.float32)
    thr_c = thr_ref[:, :1]
    tie_c = tie_ref[:, :1]

    def attend(c0, n):
        kc = jnp.concatenate([key_ref[c0 + u] for u in range(n)], axis=1)
        idx = c0 * tk + lax.broadcasted_iota(jnp.int32, (1, n * tk), 1)
        sel = (kc > thr_c) | ((kc == thr_c) & (idx <= tie_c))
        bias = jnp.where(sel & (idx <= qpos), 0.0, MASK_BIAS)
        kt = jnp.concatenate([kt_ref[0, c0 + u] for u in range(n)], axis=1)
        vv = jnp.concatenate([v_ref[0, c0 + u] for u in range(n)], axis=0)
        for h in range(HC):
            s = jnp.dot(q_ref[0, :, h * DHC:(h + 1) * DHC], kt[h * DHC:(h + 1) * DHC, :],
                        preferred_element_type=jnp.float32) + bias
            _flash_update(s, vv[:, h * DHC:(h + 1) * DHC], m_ref, l_ref, acc_ref, h, DHC)

    def group_body(g, carry):
        attend(g * att_group, att_group)
        return carry

    def single_body(c, carry):
        attend(c, 1)
        return carry

    n_groups = nkc // att_group
    lax.fori_loop(0, n_groups, group_body, 0)
    lax.fori_loop(n_groups * att_group, nkc, single_body, 0)
    o_ref[0] = jnp.concatenate([acc_ref[h] / l_ref[h][:, :DHC] for h in range(HC)], axis=1)


def dsa_prompt_pallas(q, k, v, qi, wi, ki, *, tq=256, tk=256, att_group=4):
    b, s, _ = q.shape
    topk = min(TOPK_MAX, s // 4)
    assert s % tq == 0 and tq % tk == 0 and tq >= topk
    nc = s // tk
    bf = jnp.bfloat16
    kit = ki.astype(bf).reshape(b, nc, tk, DI).swapaxes(2, 3)
    kt = k.astype(bf).reshape(b, nc, tk, WC).swapaxes(2, 3)
    vc = v.astype(bf).reshape(b, nc, tk, WC)
    body = functools.partial(_dsa_prompt_body, tq=tq, tk=tk, att_group=att_group, topk=topk,
                             idx_bits=int(s).bit_length())
    return pl.pallas_call(
        body,
        grid=(b, s // tq),
        in_specs=[
            pl.BlockSpec((1, tq, HI * DI), lambda bi, i: (bi, i, 0)),
            pl.BlockSpec((1, tq, HI), lambda bi, i: (bi, i, 0)),
            pl.BlockSpec((1, tq, WC), lambda bi, i: (bi, i, 0)),
            pl.BlockSpec((1, nc, DI, tk), lambda bi, i: (bi, 0, 0, 0)),
            pl.BlockSpec((1, nc, WC, tk), lambda bi, i: (bi, 0, 0, 0)),
            pl.BlockSpec((1, nc, tk, WC), lambda bi, i: (bi, 0, 0, 0)),
        ],
        out_specs=pl.BlockSpec((1, tq, WC), lambda bi, i: (bi, i, 0)),
        out_shape=jax.ShapeDtypeStruct((b, s, WC), jnp.float32),
        scratch_shapes=[
            pltpu.VMEM((nc, tq, tk), jnp.int32),
            pltpu.VMEM((nc, tk, tq), jnp.int32),
            pltpu.VMEM((HI, tq, LANES), jnp.float32),
            pltpu.VMEM((tq, LANES), jnp.int32),
            pltpu.VMEM((tq, LANES), jnp.int32),
            pltpu.VMEM((HC, tq, LANES), jnp.float32),
            pltpu.VMEM((HC, tq, LANES), jnp.float32),
            pltpu.VMEM((HC, tq, DHC), jnp.float32),
        ],
        compiler_params=pltpu.CompilerParams(dimension_semantics=("arbitrary", "arbitrary"),
                                             vmem_limit_bytes=VMEM_LIMIT_BYTES),
        name="dsa_prompt",
    )(qi.astype(bf), wi, q.astype(bf), kit, kt, vc)


def _diff_prompt_body(lam_ref, q_ref, kt_ref, v_ref, g_ref, o_ref, m_ref, l_ref, acc_ref,
                      *, tq, tk, out_scale):
    i = pl.program_id(1)
    qpos = i * tq + lax.broadcasted_iota(jnp.int32, (tq, 1), 0)
    m_ref[...] = jnp.full(m_ref.shape, M_INIT, jnp.float32)
    l_ref[...] = jnp.zeros(l_ref.shape, jnp.float32)
    acc_ref[...] = jnp.zeros(acc_ref.shape, jnp.float32)

    def step(c, masked):
        kt = kt_ref[0, c]
        vv = v_ref[0, c]
        if masked:
            kpos = c * tk + lax.broadcasted_iota(jnp.int32, (1, tk), 1)
            bias = jnp.where(kpos <= qpos, 0.0, MASK_BIAS)
        for h in range(HB):
            for comp in range(2):
                o = (h * 2 + comp) * DHB
                s = jnp.dot(q_ref[0, :, o:o + DHB], kt[o:o + DHB, :], preferred_element_type=jnp.float32)
                if masked:
                    s = s + bias
                _flash_update(s, vv[:, h * DVB:(h + 1) * DVB], m_ref, l_ref, acc_ref, h * 2 + comp, DVB)

    def full_body(c, carry):
        step(c, False)
        return carry

    def diag_body(c, carry):
        step(c, True)
        return carry

    n_full = lax.div(i * tq, tk)
    n_end = lax.div((i + 1) * tq + tk - 1, tk)
    lax.fori_loop(0, n_full, full_body, 0)
    lax.fori_loop(n_full, n_end, diag_body, 0)

    lam = lam_ref[0]
    outs = []
    for h in range(HB):
        o1 = acc_ref[2 * h] / l_ref[2 * h]
        o2 = acc_ref[2 * h + 1] / l_ref[2 * h + 1]
        o = o1 - lam * o2
        o = o * lax.rsqrt(jnp.mean(o * o, axis=1, keepdims=True) + DIFF_SUBLN_EPS)
        outs.append(o * g_ref[...] * out_scale)
    o_ref[0] = jnp.concatenate(outs, axis=1)


def diff_prompt_pallas(q, k, v, lam, subln_g, out_scale, *, tq=512, tk=512):
    b, s, _ = q.shape
    tq, tk = min(tq, s), min(tk, s)
    assert s % tq == 0 and s % tk == 0 and (tq % tk == 0 or tk % tq == 0) and DVB == LANES
    nc = s // tk
    bf = jnp.bfloat16
    wq = HB * 2 * DHB
    kt = k.astype(bf).reshape(b, nc, tk, wq).swapaxes(2, 3)
    vc = v.astype(bf).reshape(b, nc, tk, WB)
    body = functools.partial(_diff_prompt_body, tq=tq, tk=tk, out_scale=out_scale)
    return pl.pallas_call(
        body,
        grid=(b, s // tq),
        in_specs=[
            pl.BlockSpec(memory_space=pltpu.SMEM),
            pl.BlockSpec((1, tq, wq), lambda bi, i: (bi, i, 0)),
            pl.BlockSpec((1, nc, wq, tk), lambda bi, i: (bi, 0, 0, 0)),
            pl.BlockSpec((1, nc, tk, WB), lambda bi, i: (bi, 0, 0, 0)),
            pl.BlockSpec((1, DVB), lambda bi, i: (0, 0)),
        ],
        out_specs=pl.BlockSpec((1, tq, WB), lambda bi, i: (bi, i, 0)),
        out_shape=jax.ShapeDtypeStruct((b, s, WB), jnp.float32),
        scratch_shapes=[
            pltpu.VMEM((HB * 2, tq, LANES), jnp.float32),
            pltpu.VMEM((HB * 2, tq, LANES), jnp.float32),
            pltpu.VMEM((HB * 2, tq, DVB), jnp.float32),
        ],
        compiler_params=pltpu.CompilerParams(dimension_semantics=("arbitrary", "arbitrary"),
                                             vmem_limit_bytes=VMEM_LIMIT_BYTES),
        name="diff_prompt",
    )(lam.reshape(1).astype(jnp.float32), q.astype(bf), kt, vc, subln_g.reshape(1, DVB).astype(jnp.float32))


PAGES_PER_STEP = 16
SELECT_PAGES_PER_STEP = 32


def _order_key(x):
    x = jnp.where(x == 0.0, 0.0, x)
    bits = pltpu.bitcast(x, jnp.int32)
    return bits ^ ((bits >> 31) & 0x7FFFFFFF)


def _page_specs(block, layer, n_rep):
    def spec(g):
        return pl.BlockSpec((None, None) + block,
                            lambda b, j, pt: (layer, pt[b, j * n_rep + g], 0, 0))
    return [spec(g) for g in range(n_rep)]


def _dsa_sample_select_body(pt_ref, qi_ref, w_ref, kin_ref, *rest, pg, nj, t_new, topk, idx_bits):
    page_refs, (bias_ref, key_scr, keyn_scr) = rest[:pg], rest[pg:]
    j = pl.program_id(1)
    width = pg * PAGE_SIZE
    bf = jnp.bfloat16

    def scores(kit):
        qk = jnp.dot(qi_ref[0], kit.astype(bf), preferred_element_type=jnp.float32)
        sc = jnp.maximum(qk, 0.0) * w_ref[0]
        out = sc[0:t_new]
        for h in range(1, HI):
            out = out + sc[h * t_new:(h + 1) * t_new]
        return out

    key_scr[j] = _order_key(scores(jnp.concatenate([r[...] for r in page_refs], axis=1)))

    @pl.when(j == nj - 1)
    def _():
        past = nj * width
        tok = lax.broadcasted_iota(jnp.int32, (t_new, 1), 0)
        lane_n = lax.broadcasted_iota(jnp.int32, (1, PAGE_SIZE), 1)
        sc_n = jnp.where(lane_n <= tok, scores(kin_ref[0]), -jnp.inf)
        keyn_scr[...] = _order_key(sc_n)
        lane_p = lax.broadcasted_iota(jnp.int32, (1, width), 1)
        idx_n = past + lane_n

        def count(pred):
            part = jnp.zeros((t_new, width), jnp.int32)
            for s in range(nj):
                part = part + pred(key_scr[s], s * width + lane_p).astype(jnp.int32)
            return (jnp.sum(part, axis=1, keepdims=True)
                    + jnp.sum(pred(keyn_scr[...], idx_n).astype(jnp.int32), axis=1, keepdims=True))

        def thr_bit(bi, t):
            cand = t + jnp.left_shift(jnp.int32(1), 31 - bi)
            return jnp.where(count(lambda k, idx: k >= cand) >= topk, cand, t)

        thr = lax.fori_loop(0, 32, thr_bit, jnp.full((t_new, 1), INT32_MIN, jnp.int32))
        need = topk - count(lambda k, idx: k > thr)

        def tie_bit(bi, jb):
            cand = jb + jnp.left_shift(jnp.int32(1), idx_bits - 1 - bi)
            before = count(lambda k, idx: (k == thr) & (idx < cand))
            return jnp.where(before <= need - 1, cand, jb)

        tie = lax.fori_loop(0, idx_bits, tie_bit, jnp.zeros((t_new, 1), jnp.int32))

        def bias_of(k, idx):
            return jnp.where((k > thr) | ((k == thr) & (idx <= tie)), 0.0, MASK_BIAS)

        for s in range(nj):
            idx_s = s * width + lax.broadcasted_iota(jnp.int32, (1, width), 1)
            bias_ref[0, :, s * width:(s + 1) * width] = bias_of(key_scr[s], idx_s)
        bias_ref[0, :, past:past + PAGE_SIZE] = jnp.where(lane_n <= tok, bias_of(keyn_scr[...], idx_n), MASK_BIAS)


def _dsa_sample_attn_body(pt_ref, q_ref, bias_ref, biasn_ref, knt_ref, vnt_ref, *rest, pg, t_new):
    k_refs, v_refs = rest[:pg], rest[pg:2 * pg]
    o_ref, m_ref, l_ref, acc_ref = rest[2 * pg:]
    j = pl.program_id(1)
    bf = jnp.bfloat16

    @pl.when(j == 0)
    def _():
        m_ref[...] = jnp.full(m_ref.shape, M_INIT, jnp.float32)
        l_ref[...] = jnp.zeros(l_ref.shape, jnp.float32)
        acc_ref[...] = jnp.zeros(acc_ref.shape, jnp.float32)

    def update(kt, vt, bias):
        s = jnp.dot(q_ref[0], kt.astype(bf), preferred_element_type=jnp.float32)
        s = s + jnp.concatenate([bias] * HC, axis=0)
        m_prev = m_ref[...]
        m_new = jnp.maximum(m_prev, jnp.max(s, axis=1, keepdims=True))
        alpha = jnp.exp(m_prev - m_new)
        p = jnp.exp(s - m_new[:, :1])
        l_ref[...] = alpha * l_ref[...] + jnp.sum(p, axis=1, keepdims=True)
        pv = lax.dot_general(p.astype(bf), vt.astype(bf), (((1,), (1,)), ((), ())),
                             preferred_element_type=jnp.float32)
        acc_ref[...] = acc_ref[...] * jnp.concatenate([alpha] * (WC // LANES), axis=1) + pv
        m_ref[...] = m_new

    update(jnp.concatenate([r[...] for r in k_refs], axis=1),
           jnp.concatenate([r[...] for r in v_refs], axis=1), bias_ref[0])

    @pl.when(j == pl.num_programs(1) - 1)
    def _():
        update(knt_ref[0], vnt_ref[0], biasn_ref[0])
        head_of_lane = lax.broadcasted_iota(jnp.int32, (1, WC), 1) // DHC
        out = jnp.zeros((t_new, WC), jnp.float32)
        for h in range(HC):
            rows = slice(h * t_new, (h + 1) * t_new)
            o_h = acc_ref[rows] / jnp.concatenate([l_ref[rows]] * (WC // LANES), axis=1)
            out = jnp.where(head_of_lane == h, o_h, out)
        o_ref[0] = out


def dsa_sample_pallas(q, k_new, v_new, qi, wi, ki_new, pool_k, pool_v, pool_ki, page_table, layer,
                      *, pg=PAGES_PER_STEP, pg_select=SELECT_PAGES_PER_STEP):
    db, t_new, _ = q.shape
    n_pages = page_table.shape[1]
    past = n_pages * PAGE_SIZE
    assert n_pages % pg == 0 and n_pages % pg_select == 0 and t_new <= PAGE_SIZE
    nj, nj_s = n_pages // pg, n_pages // pg_select
    width, width_s = pg * PAGE_SIZE, pg_select * PAGE_SIZE
    total = past + PAGE_SIZE
    topk = min(TOPK_MAX, (past + t_new) // 4)
    bf = jnp.bfloat16
    n_layers, n_pool = pool_k.shape[:2]
    kit_pool = pool_ki.transpose(0, 1, 3, 2)
    kt_pool = pool_k.transpose(0, 1, 3, 4, 2).reshape(n_layers, n_pool, WC, PAGE_SIZE)
    vt_pool = pool_v.transpose(0, 1, 3, 4, 2).reshape(n_layers, n_pool, WC, PAGE_SIZE)
    pad_t = lambda x: jnp.pad(x.swapaxes(1, 2), ((0, 0), (0, 0), (0, PAGE_SIZE - t_new)))
    qi2 = qi.reshape(db, t_new, HI, DI).swapaxes(1, 2).reshape(db, HI * t_new, DI).astype(bf)
    w2 = wi.swapaxes(1, 2).reshape(db, HI * t_new, 1)
    seq_spec = lambda shape: pl.BlockSpec((1,) + shape, lambda b, j, pt: (b, 0, 0))
    bias = pl.pallas_call(
        functools.partial(_dsa_sample_select_body, pg=pg_select, nj=nj_s, t_new=t_new, topk=topk,
                          idx_bits=int(total).bit_length()),
        grid_spec=pltpu.PrefetchScalarGridSpec(
            num_scalar_prefetch=1, grid=(db, nj_s),
            in_specs=[seq_spec((HI * t_new, DI)), seq_spec((HI * t_new, 1)), seq_spec((DI, PAGE_SIZE))]
            + _page_specs((DI, PAGE_SIZE), layer, pg_select),
            out_specs=seq_spec((t_new, total)),
            scratch_shapes=[pltpu.VMEM((nj_s, t_new, width_s), jnp.int32),
                            pltpu.VMEM((t_new, PAGE_SIZE), jnp.int32)]),
        out_shape=jax.ShapeDtypeStruct((db, t_new, total), jnp.float32),
        compiler_params=pltpu.CompilerParams(dimension_semantics=("arbitrary", "arbitrary")),
        name="dsa_sample_select",
    )(page_table, qi2, w2, pad_t(ki_new).astype(bf), *([kit_pool] * pg_select))

    qh = q.reshape(db, t_new, HC, DHC).swapaxes(1, 2)
    qbd = (qh[:, :, :, None, :] * jnp.eye(HC, dtype=q.dtype)[None, :, None, :, None])
    qbd = qbd.reshape(db, HC * t_new, WC).astype(bf)
    return pl.pallas_call(
        functools.partial(_dsa_sample_attn_body, pg=pg, t_new=t_new),
        grid_spec=pltpu.PrefetchScalarGridSpec(
            num_scalar_prefetch=1, grid=(db, nj),
            in_specs=[seq_spec((HC * t_new, WC)),
                      pl.BlockSpec((1, t_new, width), lambda b, j, pt: (b, 0, j)),
                      pl.BlockSpec((1, t_new, PAGE_SIZE), lambda b, j, pt: (b, 0, n_pages)),
                      seq_spec((WC, PAGE_SIZE)), seq_spec((WC, PAGE_SIZE))]
            + _page_specs((WC, PAGE_SIZE), layer, pg) + _page_specs((WC, PAGE_SIZE), layer, pg),
            out_specs=seq_spec((t_new, WC)),
            scratch_shapes=[pltpu.VMEM((HC * t_new, LANES), jnp.float32),
                            pltpu.VMEM((HC * t_new, LANES), jnp.float32),
                            pltpu.VMEM((HC * t_new, WC), jnp.float32)]),
        out_shape=jax.ShapeDtypeStruct((db, t_new, WC), jnp.float32),
        compiler_params=pltpu.CompilerParams(dimension_semantics=("arbitrary", "arbitrary")),
        name="dsa_sample_attn",
    )(page_table, qbd, bias, bias, pad_t(k_new).astype(bf), pad_t(v_new).astype(bf),
      *([kt_pool] * pg), *([vt_pool] * pg))


def _diff_sample_body(pt_ref, lam_ref, q_ref, knt_ref, vn_ref, g_ref, *rest, pg, t_new, out_scale):
    k_refs, v_refs = rest[:pg], rest[pg:2 * pg]
    o_ref, m_ref, l_ref, acc_ref = rest[2 * pg:]
    j = pl.program_id(1)
    bf = jnp.bfloat16

    @pl.when(j == 0)
    def _():
        m_ref[...] = jnp.full(m_ref.shape, M_INIT, jnp.float32)
        l_ref[...] = jnp.zeros(l_ref.shape, jnp.float32)
        acc_ref[...] = jnp.zeros(acc_ref.shape, jnp.float32)

    def update(kt, v, bias):
        s = jnp.dot(q_ref[0], kt.astype(bf), preferred_element_type=jnp.float32)
        if bias is not None:
            s = s + bias
        m_prev = m_ref[...]
        m_new = jnp.maximum(m_prev, jnp.max(s, axis=1, keepdims=True))
        alpha = jnp.exp(m_prev - m_new)
        p = jnp.exp(s - m_new[:, :1])
        l_ref[...] = alpha * l_ref[...] + jnp.sum(p, axis=1, keepdims=True)
        pv = jnp.dot(p.astype(bf), v.astype(bf), preferred_element_type=jnp.float32)
        acc_ref[...] = acc_ref[...] * jnp.concatenate([alpha] * HB, axis=1) + pv
        m_ref[...] = m_new

    v_pages = [jnp.concatenate([r[pl.ds(h, PAGE_SIZE, stride=HB), :] for h in range(HB)], axis=1) for r in v_refs]
    update(jnp.concatenate([r[...] for r in k_refs], axis=1), jnp.concatenate(v_pages, axis=0), None)

    @pl.when(j == pl.num_programs(1) - 1)
    def _():
        n_rows = 2 * HB * t_new
        tok = lax.broadcasted_iota(jnp.int32, (n_rows, 1), 0) % t_new
        lane_n = lax.broadcasted_iota(jnp.int32, (1, PAGE_SIZE), 1)
        update(knt_ref[0], vn_ref[0], jnp.where(lane_n <= tok, 0.0, MASK_BIAS))
        lam = lam_ref[0]
        outs = []
        for h in range(HB):
            r1 = slice(2 * h * t_new, (2 * h + 1) * t_new)
            r2 = slice((2 * h + 1) * t_new, (2 * h + 2) * t_new)
            cols = slice(h * DVB, (h + 1) * DVB)
            o = acc_ref[r1, cols] / l_ref[r1] - lam * (acc_ref[r2, cols] / l_ref[r2])
            o = o * lax.rsqrt(jnp.mean(o * o, axis=1, keepdims=True) + DIFF_SUBLN_EPS)
            outs.append(o * g_ref[...] * out_scale)
        o_ref[0] = jnp.concatenate(outs, axis=1)


def diff_sample_pallas(q, k_new, v_new, pool_k, pool_v, page_table, layer, lam, subln_g, out_scale,
                       *, pg=PAGES_PER_STEP):
    db, t_new, wq = q.shape
    n_pages = page_table.shape[1]
    assert n_pages % pg == 0 and t_new <= PAGE_SIZE and DVB == LANES
    nj = n_pages // pg
    bf = jnp.bfloat16
    n_layers, n_pool = pool_k.shape[:2]
    kt_pool = pool_k.transpose(0, 1, 3, 4, 5, 2).reshape(n_layers, n_pool, wq, PAGE_SIZE)
    v_pool = pool_v.reshape(n_layers, n_pool, PAGE_SIZE * HB, DVB)
    n_hc = 2 * HB
    qh = q.reshape(db, t_new, n_hc, DHB).swapaxes(1, 2)
    qbd = (qh[:, :, :, None, :] * jnp.eye(n_hc, dtype=q.dtype)[None, :, None, :, None])
    qbd = qbd.reshape(db, n_hc * t_new, wq).astype(bf)
    knt = jnp.pad(k_new.swapaxes(1, 2), ((0, 0), (0, 0), (0, PAGE_SIZE - t_new))).astype(bf)
    vn = jnp.pad(v_new, ((0, 0), (0, PAGE_SIZE - t_new), (0, 0))).astype(bf)
    seq_spec = lambda shape: pl.BlockSpec((1,) + shape, lambda b, j, pt: (b, 0, 0))
    return pl.pallas_call(
        functools.partial(_diff_sample_body, pg=pg, t_new=t_new, out_scale=out_scale),
        grid_spec=pltpu.PrefetchScalarGridSpec(
            num_scalar_prefetch=1, grid=(db, nj),
            in_specs=[pl.BlockSpec(memory_space=pltpu.SMEM),
                      seq_spec((n_hc * t_new, wq)), seq_spec((wq, PAGE_SIZE)), seq_spec((PAGE_SIZE, WB)),
                      pl.BlockSpec((1, DVB), lambda b, j, pt: (0, 0))]
            + _page_specs((wq, PAGE_SIZE), layer, pg) + _page_specs((PAGE_SIZE * HB, DVB), layer, pg),
            out_specs=seq_spec((t_new, WB)),
            scratch_shapes=[pltpu.VMEM((n_hc * t_new, LANES), jnp.float32),
                            pltpu.VMEM((n_hc * t_new, LANES), jnp.float32),
                            pltpu.VMEM((n_hc * t_new, WB), jnp.float32)]),
        out_shape=jax.ShapeDtypeStruct((db, t_new, WB), jnp.float32),
        compiler_params=pltpu.CompilerParams(dimension_semantics=("arbitrary", "arbitrary"),
                                             vmem_limit_bytes=VMEM_LIMIT_BYTES),
        name="diff_sample",
    )(page_table, lam.reshape(1).astype(jnp.float32), qbd, knt, vn,
      subln_g.reshape(1, DVB).astype(jnp.float32), *([kt_pool] * pg), *([v_pool] * pg))


RWKV_TBLK = LANES // 2
RWKV_PAIRS = HA // 2


def _rwkv_scan_body(r_ref, w_ref, k_ref, kk_ref, b_ref, vt_ref, e_ref, s0_ref, ot_ref, sout_ref,
                    s_scr, vb_scr, sr_scr, *, bb, nsteps):
    ti = pl.program_id(1)

    @pl.when(ti == 0)
    def _():
        s_scr[...] = s0_ref[...]

    slot = lax.broadcasted_iota(jnp.int32, (1, LANES), 1) & (RWKV_TBLK - 1)
    e = e_ref[...]
    bf = jnp.bfloat16
    units = [(bi, p) for bi in range(bb) for p in range(RWKV_PAIRS)]
    n_u = len(units)

    for u, (bi, p) in enumerate(units):
        vt = vt_ref[bi, p, 0]
        lhs = jnp.concatenate([jnp.where(slot == t, vt, 0.0).astype(bf) for t in range(nsteps)], axis=0)
        vb_scr[u] = jnp.dot(lhs, e, preferred_element_type=jnp.float32).reshape(nsteps, NA, LANES)

    def step(t, carry):
        rows = [[ref[bi, pl.ds(t, 1), :] for ref in (r_ref, w_ref, k_ref, kk_ref, b_ref)] for bi in range(bb)]
        ms = [(s_scr[bi, p] * rows[bi][3][:, p * LANES:(p + 1) * LANES]).astype(bf) for bi, p in units]
        sk_all = jnp.dot(jnp.concatenate(ms, axis=0), e, preferred_element_type=jnp.float32)
        for u, (bi, p) in enumerate(units):
            cols = slice(p * LANES, (p + 1) * LANES)
            r_t, w_t, k_t, _, b_t = rows[bi]
            sk = sk_all[u * NA:(u + 1) * NA]
            s = s_scr[bi, p] * w_t[:, cols] - sk * b_t[:, cols] + vb_scr[u, t] * k_t[:, cols]
            s_scr[bi, p] = s
            sr_scr[u, t] = s * r_t[:, cols]
        return carry

    lax.fori_loop(0, nsteps, step, 0)

    for u, (bi, p) in enumerate(units):
        o_b = jnp.dot(sr_scr[u].reshape(nsteps * NA, LANES).astype(bf), e, preferred_element_type=jnp.float32)
        ot = jnp.zeros((NA, LANES), jnp.float32)
        for t in range(nsteps):
            ot = jnp.where(slot == t, o_b[t * NA:(t + 1) * NA], ot)
        ot_ref[bi, p, 0] = ot

    @pl.when(ti == pl.num_programs(1) - 1)
    def _():
        sout_ref[...] = s_scr[...]


def rwkv_scan_pallas(r, w, k, v, kk, b, s0, *, bb=2):
    nb, t, _ = r.shape
    nsteps = min(RWKV_TBLK, t)
    assert nb % bb == 0 and t % nsteps == 0 and LANES == 2 * NA
    nblk = t // nsteps
    vt = v.reshape(nb, nblk, nsteps, RWKV_PAIRS, 2, NA).transpose(0, 3, 1, 5, 4, 2)
    vt = jnp.pad(vt, ((0, 0),) * 5 + ((0, RWKV_TBLK - nsteps),)).reshape(nb, RWKV_PAIRS, nblk, NA, LANES)
    s0p = s0.reshape(nb, RWKV_PAIRS, 2, NA, NA).transpose(0, 1, 3, 2, 4).reshape(nb, RWKV_PAIRS, NA, LANES)
    head_of_lane = jnp.arange(LANES) // NA
    e = (head_of_lane[:, None] == head_of_lane[None, :]).astype(jnp.bfloat16)
    row_spec = pl.BlockSpec((bb, nsteps, WA), lambda bi, ti: (bi, ti, 0))
    col_spec = pl.BlockSpec((bb, RWKV_PAIRS, 1, NA, LANES), lambda bi, ti: (bi, 0, ti, 0, 0))
    st_spec = pl.BlockSpec((bb, RWKV_PAIRS, NA, LANES), lambda bi, ti: (bi, 0, 0, 0))
    ot, s_fin = pl.pallas_call(
        functools.partial(_rwkv_scan_body, bb=bb, nsteps=nsteps),
        grid=(nb // bb, nblk),
        in_specs=[row_spec, row_spec, row_spec, row_spec, row_spec, col_spec,
                  pl.BlockSpec((LANES, LANES), lambda bi, ti: (0, 0)), st_spec],
        out_specs=[col_spec, st_spec],
        out_shape=[jax.ShapeDtypeStruct((nb, RWKV_PAIRS, nblk, NA, LANES), jnp.float32),
                   jax.ShapeDtypeStruct((nb, RWKV_PAIRS, NA, LANES), jnp.float32)],
        scratch_shapes=[pltpu.VMEM((bb, RWKV_PAIRS, NA, LANES), jnp.float32),
                        pltpu.VMEM((bb * RWKV_PAIRS, nsteps, NA, LANES), jnp.float32),
                        pltpu.VMEM((bb * RWKV_PAIRS, nsteps, NA, LANES), jnp.float32)],
        compiler_params=pltpu.CompilerParams(dimension_semantics=("arbitrary", "arbitrary"),
                                             vmem_limit_bytes=VMEM_LIMIT_BYTES),
        name="rwkv_scan",
    )(r, w, k, kk, b, vt, e, s0p)
    o = ot.reshape(nb, RWKV_PAIRS, nblk, NA, 2, RWKV_TBLK)[..., :nsteps]
    o = o.transpose(0, 2, 5, 1, 4, 3).reshape(nb, t, WA)
    s_fin = s_fin.reshape(nb, RWKV_PAIRS, NA, 2, NA).transpose(0, 1, 3, 2, 4).reshape(nb, HA, NA, NA)
    return o, s_fin


ROW_TILE = 256
N_MAIN = NCA + NCB + NCC
N_MAIN_PAD = -(-N_MAIN // LANES) * LANES
PROJ_SEGMENTS = (
    ("pa", 0, NCA, jnp.float32, 1.0),
    ("qb", NCA, HB * 2 * DHB, jnp.bfloat16, DHB ** -0.5),
    ("kb", NCA + HB * 2 * DHB, HB * 2 * DHB, jnp.float32, 1.0),
    ("vb", NCA + 2 * HB * 2 * DHB, WB, jnp.float32, 1.0),
    ("qc", NCA + NCB, WC, jnp.bfloat16, DHC ** -0.5),
    ("kc", NCA + NCB + WC, WC, jnp.float32, 1.0),
    ("vc", NCA + NCB + 2 * WC, WC, jnp.float32, 1.0),
    ("qi", NCA + NCB + 3 * WC, HI * DI, jnp.bfloat16, DI ** -0.5),
)
KI_START = NCA + NCB + 3 * WC + HI * DI


def _proj_body(x_ref, g_ref, wm_ref, wg_ref, *out_refs):
    seg_refs, (ki_ref, wi_ref, gate_ref) = out_refs[:len(PROJ_SEGMENTS)], out_refs[len(PROJ_SEGMENTS):]
    x = x_ref[...]
    xn = (x * lax.rsqrt(jnp.mean(x * x, axis=-1, keepdims=True) + NORM_EPS) * g_ref[...]).astype(jnp.bfloat16)
    for (name, start, width, dtype, scale), ref in zip(PROJ_SEGMENTS, seg_refs):
        y = jnp.dot(xn, wm_ref[:, start:start + width], preferred_element_type=jnp.float32)
        ref[...] = (y * scale if scale != 1.0 else y).astype(dtype)
    tail = jnp.dot(xn, wm_ref[:, KI_START:KI_START + LANES], preferred_element_type=jnp.float32)
    ki_ref[...] = tail[:, :DI]
    wi_ref[...] = tail[:, DI:DI + HI] * (HI ** -0.5)
    for c in range(N_BRANCH):
        y = jnp.dot(xn, wg_ref[:, c * D_MODEL:(c + 1) * D_MODEL], preferred_element_type=jnp.float32)
        gate_ref[:, c * D_MODEL:(c + 1) * D_MODEL] = jax.nn.sigmoid(y)


def proj_pallas(x2d, g, w_in_l):
    t, d = x2d.shape
    tm = min(ROW_TILE, t)
    assert t % tm == 0 and KI_START % LANES == 0 and KI_START + DI + HI == N_MAIN
    bf = jnp.bfloat16
    w_main = jnp.pad(w_in_l[:, :N_MAIN], ((0, 0), (0, N_MAIN_PAD - N_MAIN))).astype(bf)
    w_gate = w_in_l[:, N_MAIN:].astype(bf)
    row = lambda width: pl.BlockSpec((tm, width), lambda i: (i, 0))
    full = lambda a: pl.BlockSpec(a.shape, lambda i: (0, 0))
    widths = [s[2] for s in PROJ_SEGMENTS] + [DI, HI, N_BRANCH * D_MODEL]
    dtypes = [s[3] for s in PROJ_SEGMENTS] + [jnp.float32] * 3
    outs = pl.pallas_call(
        _proj_body,
        grid=(t // tm,),
        in_specs=[row(d), pl.BlockSpec((1, d), lambda i: (0, 0)), full(w_main), full(w_gate)],
        out_specs=[row(w) for w in widths],
        out_shape=[jax.ShapeDtypeStruct((t, w), dt) for w, dt in zip(widths, dtypes)],
        compiler_params=pltpu.CompilerParams(dimension_semantics=("arbitrary",),
                                             vmem_limit_bytes=VMEM_LIMIT_BYTES),
        name="in_proj",
    )(x2d, g.reshape(1, d), w_main, w_gate)
    names = [s[0] for s in PROJ_SEGMENTS] + ["ki", "wi", "gates"]
    return dict(zip(names, outs))


def _head_sum(x, e):
    x_hi = x.astype(jnp.bfloat16)
    x_lo = (x - x_hi.astype(jnp.float32)).astype(jnp.bfloat16)
    return (jnp.dot(x_hi, e, preferred_element_type=jnp.float32)
            + jnp.dot(x_lo, e, preferred_element_type=jnp.float32))


def _head_ones():
    head_of_lane = jnp.arange(WA) // NA
    return (head_of_lane[:, None] == head_of_lane[None, :]).astype(jnp.bfloat16)


def _rwkv_pre_body(p_ref, ps_ref, mu_ref, w0_ref, wup_ref, a0_ref, aup_ref, gup_ref, kk_ref, ka_ref, rk_ref,
                   e_ref, r_o, w_o, k_o, v_o, kk_o, b_o, g_o, bonus_o):
    bf = jnp.bfloat16
    e = e_ref[...]
    p = p_ref[...]
    px = p + (ps_ref[...] - p) * mu_ref[...]
    r, k, v = px[:, 0:WA], px[:, WA:2 * WA], px[:, 2 * WA:3 * WA]
    o = 3 * WA
    wd, ad, gd = px[:, o:o + LORA_W], px[:, o + LORA_W:o + LORA_W + LORA_A], px[:, o + LORA_W + LORA_A:]
    dot = lambda x, w_ref: jnp.dot(x.astype(bf), w_ref[...], preferred_element_type=jnp.float32)
    z = -(w0_ref[...] + dot(jnp.tanh(wd), wup_ref))
    softplus = jnp.maximum(z, 0.0) + jnp.log(1.0 + jnp.exp(-jnp.abs(z)))
    decay = jnp.exp(-jnp.exp(-softplus - 0.5))
    a = jax.nn.sigmoid(a0_ref[...] + dot(ad, aup_ref))
    kk = k * kk_ref[...]
    kk = kk * lax.rsqrt(jnp.maximum(_head_sum(kk * kk, e), 1e-24))
    k = k * (1.0 + (a - 1.0) * ka_ref[...])
    r_o[...] = r
    w_o[...] = decay
    k_o[...] = k
    v_o[...] = v
    kk_o[...] = kk
    b_o[...] = kk * a
    g_o[...] = dot(jax.nn.sigmoid(gd), gup_ref)
    bonus_o[...] = _head_sum(r * k * rk_ref[...], e) * v


def rwkv_pre_pallas(pa2d, ps2d, mu, w0, w_up, a0, a_up, g_up, k_k, k_a, r_k):
    t, _ = pa2d.shape
    tm = min(ROW_TILE, t)
    bf = jnp.bfloat16
    vecs = [x.reshape(1, -1).astype(jnp.float32) for x in (mu, w0, a0, k_k, k_a, r_k)]
    mats = [x.astype(bf) for x in (w_up, a_up, g_up)]
    row = lambda width: pl.BlockSpec((tm, width), lambda i: (i, 0))
    full = lambda a: pl.BlockSpec(a.shape, lambda i: (0, 0))
    args = [pa2d, ps2d, vecs[0], vecs[1], mats[0], vecs[2], mats[1], mats[2], vecs[3], vecs[4], vecs[5], _head_ones()]
    return pl.pallas_call(
        _rwkv_pre_body,
        grid=(t // tm,),
        in_specs=[row(NCA), row(NCA)] + [full(a) for a in args[2:]],
        out_specs=[row(WA)] * 8,
        out_shape=[jax.ShapeDtypeStruct((t, WA), jnp.float32)] * 8,
        compiler_params=pltpu.CompilerParams(dimension_semantics=("arbitrary",)),
        name="rwkv_pre",
    )(*args)


def _merge_body(h_ref, o_ref, bonus_ref, g_ref, lnw_ref, lnb_ref, e_ref, ob_ref, oc_ref, gate_ref,
                wa_ref, wb_ref, wc_ref, wo_ref, out_ref):
    bf = jnp.bfloat16
    e = e_ref[...]
    o = o_ref[...]
    c = o - _head_sum(o, e) * (1.0 / NA)
    var = _head_sum(c * c, e) * (1.0 / NA)
    oa = (c * lax.rsqrt(var + RWKV_GN_EPS) * lnw_ref[...] + lnb_ref[...] + bonus_ref[...]) * g_ref[...]
    merged = None
    for ci, (x, w_ref) in enumerate(((oa, wa_ref), (ob_ref[...], wb_ref), (oc_ref[...], wc_ref))):
        br = jnp.dot(x.astype(bf), w_ref[...], preferred_element_type=jnp.float32)
        term = gate_ref[:, ci * D_MODEL:(ci + 1) * D_MODEL] * br
        merged = term if merged is None else merged + term
    out_ref[...] = h_ref[...] + jnp.dot(merged.astype(bf), wo_ref[...], preferred_element_type=jnp.float32)


def merge_pallas(h2d, o_rwkv, bonus, g_rwkv, ln_w, ln_b, ob, oc, gates, w_a, w_b, w_c, w_o):
    t, d = h2d.shape
    tm = min(ROW_TILE, t)
    bf = jnp.bfloat16
    ws = [w.astype(bf) for w in (w_a, w_b, w_c, w_o)]
    row = lambda a: pl.BlockSpec((tm, a.shape[1]), lambda i: (i, 0))
    full = lambda a: pl.BlockSpec(a.shape, lambda i: (0, 0))
    lnw, lnb, e = ln_w.reshape(1, WA), ln_b.reshape(1, WA), _head_ones()
    return pl.pallas_call(
        _merge_body,
        grid=(t // tm,),
        in_specs=[row(a) for a in (h2d, o_rwkv, bonus, g_rwkv)] + [full(a) for a in (lnw, lnb, e)]
        + [row(a) for a in (ob, oc, gates)] + [full(w) for w in ws],
        out_specs=row(h2d),
        out_shape=jax.ShapeDtypeStruct((t, d), jnp.float32),
        compiler_params=pltpu.CompilerParams(dimension_semantics=("arbitrary",),
                                             vmem_limit_bytes=VMEM_LIMIT_BYTES),
        name="branch_merge",
    )(h2d, o_rwkv, bonus, g_rwkv, lnw, lnb, e, ob, oc, gates, *ws)


FFN_CHUNK = 256


def _ffn_body(h_ref, g_ref, wg_ref, wu_ref, wd_ref, gf_ref, o_ref, acc_ref, *, final_norm):
    bf = jnp.bfloat16
    h = h_ref[...]
    xn = (h * lax.rsqrt(jnp.mean(h * h, axis=-1, keepdims=True) + NORM_EPS) * g_ref[...]).astype(bf)
    acc_ref[...] = h
    for f in range(0, D_FF, FFN_CHUNK):
        gate = jnp.dot(xn, wg_ref[:, f:f + FFN_CHUNK], preferred_element_type=jnp.float32)
        up = jnp.dot(xn, wu_ref[:, f:f + FFN_CHUNK], preferred_element_type=jnp.float32)
        act = (gate * jax.nn.sigmoid(gate) * up).astype(bf)
        acc_ref[...] += jnp.dot(act, wd_ref[f:f + FFN_CHUNK, :], preferred_element_type=jnp.float32)
    y = acc_ref[...]
    if final_norm:
        y = y * lax.rsqrt(jnp.mean(y * y, axis=-1, keepdims=True) + NORM_EPS) * gf_ref[...]
    o_ref[...] = y


def ffn_pallas(h2d, g, w_gate, w_up, w_down, g_final, *, final_norm):
    t, d = h2d.shape
    tm = min(ROW_TILE, t)
    assert D_FF % FFN_CHUNK == 0
    bf = jnp.bfloat16
    ws = [w.astype(bf) for w in (w_gate, w_up, w_down)]
    row = pl.BlockSpec((tm, d), lambda i: (i, 0))
    vec = pl.BlockSpec((1, d), lambda i: (0, 0))
    full = lambda a: pl.BlockSpec(a.shape, lambda i: (0, 0))
    return pl.pallas_call(
        functools.partial(_ffn_body, final_norm=final_norm),
        grid=(t // tm,),
        in_specs=[row, vec] + [full(w) for w in ws] + [vec],
        out_specs=row,
        out_shape=jax.ShapeDtypeStruct((t, d), jnp.float32),
        scratch_shapes=[pltpu.VMEM((tm, d), jnp.float32)],
        compiler_params=pltpu.CompilerParams(dimension_semantics=("arbitrary",),
                                             vmem_limit_bytes=VMEM_LIMIT_BYTES),
        name="ffn",
    )(h2d, g.reshape(1, d), *ws, g_final.reshape(1, d))


def _layer_group(h, shift_prev, state0, layer, attend, p):
    b, t, d = h.shape
    n = b * t
    pr = proj_pallas(h.reshape(n, d), p["g_mix"][layer], p["w_in"][layer])
    pa = pr["pa"].reshape(b, t, NCA)
    pa_shift = jnp.concatenate([shift_prev[:, None], pa[:, :-1]], axis=1)
    r, decay, k, v, kk, kk_a, g_rwkv, bonus = rwkv_pre_pallas(
        pr["pa"], pa_shift.reshape(n, NCA), p["rwkv_mu"][layer], p["rwkv_w0"][layer], p["rwkv_w_up"][layer],
        p["rwkv_a0"][layer], p["rwkv_a_up"][layer], p["rwkv_g_up"][layer], p["rwkv_k_k"][layer],
        p["rwkv_k_a"][layer], p["rwkv_r_k"][layer])
    seq = lambda x: x.reshape(b, t, WA)
    o_rwkv, s_fin = rwkv_scan_pallas(seq(r), seq(decay), seq(k), seq(v), seq(kk), seq(kk_a), state0)
    ob, oc = attend(pr)
    h2 = merge_pallas(h.reshape(n, d), o_rwkv.reshape(n, WA), bonus, g_rwkv, p["rwkv_ln_w"][layer],
                      p["rwkv_ln_b"][layer], ob, oc, pr["gates"], p["w_br_a"][layer], p["w_br_b"][layer],
                      p["w_br_c"][layer], p["w_out"][layer])
    h3 = ffn_pallas(h2, p["g_ffn"][layer], p["w_ffn_gate"][layer], p["w_ffn_up"][layer], p["w_ffn_down"][layer],
                    p["g_final"], final_norm=(layer == DEPTH - 1))
    caches = (s_fin, pa[:, -1], pr["kb"].reshape(b, t, HB, 2, DHB), pr["vb"].reshape(b, t, HB, DVB),
              pr["kc"].reshape(b, t, HC, DHC), pr["vc"].reshape(b, t, HC, DHC), pr["ki"].reshape(b, t, DI))
    return h3.reshape(b, t, d), caches


def kernel(x_prompt, x_sample, state_rwkv, state_shift, cache_diff_k, cache_diff_v, cache_dsa_k,
           cache_dsa_v, cache_idx_k, page_table, g_mix, w_in, rwkv_mu, rwkv_w0, rwkv_w_up, rwkv_a0,
           rwkv_a_up, rwkv_g_up, rwkv_k_k, rwkv_k_a, rwkv_r_k, rwkv_ln_w, rwkv_ln_b, diff_lam_q1,
           diff_lam_k1, diff_lam_q2, diff_lam_k2, diff_subln_g, w_br_a, w_br_b, w_br_c, w_out, g_ffn,
           w_ffn_gate, w_ffn_up, w_ffn_down, g_final):
    p = dict(g_mix=g_mix, w_in=w_in, rwkv_mu=rwkv_mu, rwkv_w0=rwkv_w0, rwkv_w_up=rwkv_w_up, rwkv_a0=rwkv_a0,
             rwkv_a_up=rwkv_a_up, rwkv_g_up=rwkv_g_up, rwkv_k_k=rwkv_k_k, rwkv_k_a=rwkv_k_a, rwkv_r_k=rwkv_r_k,
             rwkv_ln_w=rwkv_ln_w, rwkv_ln_b=rwkv_ln_b, w_br_a=w_br_a, w_br_b=w_br_b, w_br_c=w_br_c, w_out=w_out,
             g_ffn=g_ffn, w_ffn_gate=w_ffn_gate, w_ffn_up=w_ffn_up, w_ffn_down=w_ffn_down, g_final=g_final)
    bp, tp = x_prompt.shape[:2]
    bs, ts = x_sample.shape[:2]
    newp, news = [], []
    hp, hs = x_prompt, x_sample
    for l in range(DEPTH):
        lam_init = 0.8 - 0.6 * math.exp(-0.3 * l)
        lam = (jnp.exp(jnp.sum(diff_lam_q1[l] * diff_lam_k1[l])) - jnp.exp(jnp.sum(diff_lam_q2[l] * diff_lam_k2[l]))
               + lam_init).astype(jnp.float32)

        def attend_prompt(pr):
            seq = lambda x: x.reshape(bp, tp, -1)
            ob = diff_prompt_pallas(seq(pr["qb"]), seq(pr["kb"]), seq(pr["vb"]), lam, diff_subln_g[l], 1.0 - lam_init)
            oc = dsa_prompt_pallas(seq(pr["qc"]), seq(pr["kc"]), seq(pr["vc"]), seq(pr["qi"]), seq(pr["wi"]),
                                   seq(pr["ki"]))
            return ob.reshape(bp * tp, WB), oc.reshape(bp * tp, WC)

        def attend_sample(pr):
            seq = lambda x: x.reshape(bs, ts, -1)
            ob = diff_sample_pallas(seq(pr["qb"]), seq(pr["kb"]), seq(pr["vb"]), cache_diff_k, cache_diff_v,
                                    page_table, l, lam, diff_subln_g[l], 1.0 - lam_init)
            oc = dsa_sample_pallas(seq(pr["qc"]), seq(pr["kc"]), seq(pr["vc"]), seq(pr["qi"]), seq(pr["wi"]),
                                   seq(pr["ki"]), cache_dsa_k, cache_dsa_v, cache_idx_k, page_table, l)
            return ob.reshape(bs * ts, WB), oc.reshape(bs * ts, WC)

        hp, cp = _layer_group(hp, jnp.zeros((bp, NCA), jnp.float32), jnp.zeros((bp, HA, NA, NA), jnp.float32),
                              l, attend_prompt, p)
        newp.append(cp)
        hs, cs = _layer_group(hs, state_shift[l], state_rwkv[l], l, attend_sample, p)
        news.append(cs)

    stack = lambda group: tuple(jnp.stack([layer[i] for layer in group]) for i in range(7))
    return (hp, hs) + stack(newp) + stack(news)
```

```python
import functools
import math
import jax
import jax.numpy as jnp
from jax import lax
from jax.experimental import pallas as pl
from jax.experimental.pallas import tpu as pltpu

D_MODEL = 1024
DEPTH = 2
PAGE_SIZE = 128

HA = 4
NA = 64
WA = HA * NA
LORA_W = 64
LORA_A = 64
LORA_G = 128
NCA = 3 * WA + LORA_W + LORA_A + LORA_G
RWKV_GN_EPS = 64e-5
HB = 4
DHB = 64
DVB = 2 * DHB
WB = HB * DVB
NCB = 2 * (HB * 2 * DHB) + WB
DIFF_SUBLN_EPS = 1e-5
HC = 4
DHC = 64
WC = HC * DHC
HI = 8
DI = 64
TOPK_MAX = 256
NCC = 3 * WC + HI * DI + DI + HI
N_BRANCH = 3
N_IN = NCA + NCB + NCC + N_BRANCH * D_MODEL
D_FF = -(-8 * D_MODEL // (3 * 256)) * 256
NORM_EPS = 1e-6

VMEM_LIMIT_BYTES = 56 * 1024 * 1024
LANES = 128
MASK_BIAS = -1e30
M_INIT = -0.5e30
INT32_MIN = -(2 ** 31)


def _flash_update(s, v_bf16, m_ref, l_ref, acc_ref, slot, dv):
    m_prev = m_ref[slot]
    m_new = jnp.maximum(m_prev, jnp.max(s, axis=1, keepdims=True))
    alpha = jnp.exp(m_prev - m_new)
    p = jnp.exp(s - m_new[:, :1])
    l_ref[slot] = alpha * l_ref[slot] + jnp.sum(p, axis=1, keepdims=True)
    pv = jnp.dot(p.astype(jnp.bfloat16), v_bf16, preferred_element_type=jnp.float32)
    acc_ref[slot] = acc_ref[slot] * alpha[:, :dv] + pv
    m_ref[slot] = m_new


def _dsa_prompt_body(qi_ref, wi_ref, q_ref, kit_ref, kt_ref, v_ref, o_ref,
                     key_ref, keyhi_ref, keylo_ref, wb_ref, thr_ref, tie_ref, m_ref, l_ref, acc_ref, *, tq, tk, att_group, topk,
                     idx_bits):
    i = pl.program_id(1)
    nkc = (i + 1) * (tq // tk)
    qpos = i * tq + lax.broadcasted_iota(jnp.int32, (tq, 1), 0)

    for h in range(HI):
        wb_ref[h] = jnp.broadcast_to(wi_ref[0, :, h:h + 1], (tq, LANES))

    def score_body(c, carry):
        kit = kit_ref[0, c]
        tiles = [jnp.zeros((tq, LANES), jnp.float32)] * (tk // LANES)
        for h in range(HI):
            qk = jnp.dot(qi_ref[0, :, h * DI:(h + 1) * DI], kit, preferred_element_type=jnp.float32)
            w_b = wb_ref[h]
            tiles = [a + w_b * jnp.maximum(qk[:, j * LANES:(j + 1) * LANES], 0.0) for j, a in enumerate(tiles)]
        acc = jnp.concatenate(tiles, axis=1)
        kpos = c * tk + lax.broadcasted_iota(jnp.int32, (1, tk), 1)
        key = _order_key(jnp.where(kpos <= qpos, acc, -jnp.inf))
        key_ref[c] = key
        key_t = jnp.transpose(key)
        keyhi_ref[c] = (key_t >> 16).astype(jnp.int16)
        keylo_ref[c] = ((key_t & 0xFFFF) - 2 ** 15).astype(jnp.int16)
        return carry

    lax.fori_loop(0, nkc, score_body, 0)

    rows16 = 16
    i16 = jnp.int16

    def count(pred):
        def cbody(c, part):
            idx = (c * tk + lax.broadcasted_iota(jnp.int32, (tk, 1), 0)).astype(i16)
            hit = pred(keyhi_ref[c], keylo_ref[c], idx).astype(i16)
            for g in range(tk // rows16):
                part = part + hit[g * rows16:(g + 1) * rows16]
            return part
        part = lax.fori_loop(0, nkc, cbody, jnp.zeros((rows16, tq), i16))
        return jnp.sum(part.astype(jnp.int32), axis=0, keepdims=True)

    def search16(count_ge, target, n_bits=16):
        def bit(bi, t):
            cand = t + jnp.left_shift(jnp.int32(1), n_bits - 1 - bi)
            return jnp.where(count_ge(cand.astype(i16)) >= target, cand, t)
        return lax.fori_loop(0, n_bits, bit, jnp.full((1, tq), -(2 ** (n_bits - 1)), jnp.int32))

    thr_hi = search16(lambda v: count(lambda hi, lo, idx: hi >= v), topk)
    hi16 = thr_hi.astype(i16)
    n_above = count(lambda hi, lo, idx: hi > hi16)
    thr_lo = search16(lambda v: count(lambda hi, lo, idx: (hi == hi16) & (lo >= v)), topk - n_above)
    lo16 = thr_lo.astype(i16)
    thr = jnp.left_shift(thr_hi, 16) | ((thr_lo + 2 ** 15) & 0xFFFF)
    n_gt = n_above + count(lambda hi, lo, idx: (hi == hi16) & (lo > lo16))
    n_ge = n_above + count(lambda hi, lo, idx: (hi == hi16) & (lo >= lo16))
    need = topk - n_gt

    def tie_search():
        def tie_bit(bi, jbound):
            cand = jbound + jnp.left_shift(jnp.int32(1), idx_bits - 1 - bi)
            c16 = cand.astype(i16)
            before = count(lambda hi, lo, idx: (hi == hi16) & (lo == lo16) & (idx < c16))
            return jnp.where(before <= need - 1, cand, jbound)
        return lax.fori_loop(0, idx_bits, tie_bit, jnp.zeros((1, tq), jnp.int32))

    tie = lax.cond(jnp.max(n_ge) > topk, tie_search, lambda: jnp.full((1, tq), 2 ** 31 - 1, jnp.int32))
    thr_ref[...] = jnp.transpose(jnp.broadcast_to(thr, (LANES, tq)))
    tie_ref[...] = jnp.transpose(jnp.broadcast_to(tie, (LANES, tq)))

    m_ref[...] = jnp.full(m_ref.shape, M_INIT, jnp.float32)
    l_ref[...] = jnp.zeros(l_ref.shape, jnp.float32)
    acc_ref[...] = jnp.zeros(acc_ref.shape, jnp.float32)
    thr_c = thr_ref[:, :1]
    tie_c = tie_ref[:, :1]

    def attend(c0, n):
        kc = jnp.concatenate([key_ref[c0 + u] for u in range(n)], axis=1)
        idx = c0 * tk + lax.broadcasted_iota(jnp.int32, (1, n * tk), 1)
        sel = (kc > thr_c) | ((kc == thr_c) & (idx <= tie_c))
        bias = jnp.where(sel & (idx <= qpos), 0.0, MASK_BIAS)
        kt = jnp.concatenate([kt_ref[0, c0 + u] for u in range(n)], axis=1)
        vv = jnp.concatenate([v_ref[0, c0 + u] for u in range(n)], axis=0)
        for h in range(HC):
            s = jnp.dot(q_ref[0, :, h * DHC:(h + 1) * DHC], kt[h * DHC:(h + 1) * DHC, :],
                        preferred_element_type=jnp.float32) + bias
            _flash_update(s, vv[:, h * DHC:(h + 1) * DHC], m_ref, l_ref, acc_ref, h, DHC)

    def group_body(g, carry):
        attend(g * att_group, att_group)
        return carry

    def single_body(c, carry):
        attend(c, 1)
        return carry

    n_groups = nkc // att_group
    lax.fori_loop(0, n_groups, group_body, 0)
    lax.fori_loop(n_groups * att_group, nkc, single_body, 0)
    o_ref[0] = jnp.concatenate([acc_ref[h] / l_ref[h][:, :DHC] for h in range(HC)], axis=1)


def dsa_prompt_pallas(q, k, v, qi, wi, ki, *, tq=256, tk=256, att_group=4):
    b, s, _ = q.shape
    topk = min(TOPK_MAX, s // 4)
    assert s % tq == 0 and tq % tk == 0 and tq >= topk and s < 2 ** 15
    nc = s // tk
    bf = jnp.bfloat16
    kit = ki.astype(bf).reshape(b, nc, tk, DI).swapaxes(2, 3)
    kt = k.astype(bf).reshape(b, nc, tk, WC).swapaxes(2, 3)
    vc = v.astype(bf).reshape(b, nc, tk, WC)
    body = functools.partial(_dsa_prompt_body, tq=tq, tk=tk, att_group=att_group, topk=topk,
                             idx_bits=int(s).bit_length())
    return pl.pallas_call(
        body,
        grid=(b, s // tq),
        in_specs=[
            pl.BlockSpec((1, tq, HI * DI), lambda bi, i: (bi, i, 0)),
            pl.BlockSpec((1, tq, HI), lambda bi, i: (bi, i, 0)),
            pl.BlockSpec((1, tq, WC), lambda bi, i: (bi, i, 0)),
            pl.BlockSpec((1, nc, DI, tk), lambda bi, i: (bi, 0, 0, 0)),
            pl.BlockSpec((1, nc, WC, tk), lambda bi, i: (bi, 0, 0, 0)),
            pl.BlockSpec((1, nc, tk, WC), lambda bi, i: (bi, 0, 0, 0)),
        ],
        out_specs=pl.BlockSpec((1, tq, WC), lambda bi, i: (bi, i, 0)),
        out_shape=jax.ShapeDtypeStruct((b, s, WC), jnp.float32),
        scratch_shapes=[
            pltpu.VMEM((nc, tq, tk), jnp.int32),
            pltpu.VMEM((nc, tk, tq), jnp.int16),
            pltpu.VMEM((nc, tk, tq), jnp.int16),
            pltpu.VMEM((HI, tq, LANES), jnp.float32),
            pltpu.VMEM((tq, LANES), jnp.int32),
            pltpu.VMEM((tq, LANES), jnp.int32),
            pltpu.VMEM((HC, tq, LANES), jnp.float32),
            pltpu.VMEM((HC, tq, LANES), jnp.float32),
            pltpu.VMEM((HC, tq, DHC), jnp.float32),
        ],
        compiler_params=pltpu.CompilerParams(dimension_semantics=("arbitrary", "arbitrary"),
                                             vmem_limit_bytes=VMEM_LIMIT_BYTES),
        name="dsa_prompt",
    )(qi.astype(bf), wi, q.astype(bf), kit, kt, vc)


def _diff_prompt_body(lam_ref, q_ref, kt_ref, v_ref, g_ref, o_ref, m_ref, l_ref, acc_ref,
                      *, tq, tk, out_scale):
    i = pl.program_id(1)
    qpos = i * tq + lax.broadcasted_iota(jnp.int32, (tq, 1), 0)
    m_ref[...] = jnp.full(m_ref.shape, M_INIT, jnp.float32)
    l_ref[...] = jnp.zeros(l_ref.shape, jnp.float32)
    acc_ref[...] = jnp.zeros(acc_ref.shape, jnp.float32)

    def step(c, masked):
        kt = kt_ref[0, c]
        vv = v_ref[0, c]
        if masked:
            kpos = c * tk + lax.broadcasted_iota(jnp.int32, (1, tk), 1)
            bias = jnp.where(kpos <= qpos, 0.0, MASK_BIAS)
        for h in range(HB):
            for comp in range(2):
                o = (h * 2 + comp) * DHB
                s = jnp.dot(q_ref[0, :, o:o + DHB], kt[o:o + DHB, :], preferred_element_type=jnp.float32)
                if masked:
                    s = s + bias
                _flash_update(s, vv[:, h * DVB:(h + 1) * DVB], m_ref, l_ref, acc_ref, h * 2 + comp, DVB)

    def full_body(c, carry):
        step(c, False)
        return carry

    def diag_body(c, carry):
        step(c, True)
        return carry

    n_full = lax.div(i * tq, tk)
    n_end = lax.div((i + 1) * tq + tk - 1, tk)
    lax.fori_loop(0, n_full, full_body, 0)
    lax.fori_loop(n_full, n_end, diag_body, 0)

    lam = lam_ref[0]
    outs = []
    for h in range(HB):
        o1 = acc_ref[2 * h] / l_ref[2 * h]
        o2 = acc_ref[2 * h + 1] / l_ref[2 * h + 1]
        o = o1 - lam * o2
        o = o * lax.rsqrt(jnp.mean(o * o, axis=1, keepdims=True) + DIFF_SUBLN_EPS)
        outs.append(o * g_ref[...] * out_scale)
    o_ref[0] = jnp.concatenate(outs, axis=1)


def diff_prompt_pallas(q, k, v, lam, subln_g, out_scale, *, tq=512, tk=512):
    b, s, _ = q.shape
    tq, tk = min(tq, s), min(tk, s)
    assert s % tq == 0 and s % tk == 0 and (tq % tk == 0 or tk % tq == 0) and DVB == LANES
    nc = s // tk
    bf = jnp.bfloat16
    wq = HB * 2 * DHB
    kt = k.astype(bf).reshape(b, nc, tk, wq).swapaxes(2, 3)
    vc = v.astype(bf).reshape(b, nc, tk, WB)
    body = functools.partial(_diff_prompt_body, tq=tq, tk=tk, out_scale=out_scale)
    return pl.pallas_call(
        body,
        grid=(b, s // tq),
        in_specs=[
            pl.BlockSpec(memory_space=pltpu.SMEM),
            pl.BlockSpec((1, tq, wq), lambda bi, i: (bi, i, 0)),
            pl.BlockSpec((1, nc, wq, tk), lambda bi, i: (bi, 0, 0, 0)),
            pl.BlockSpec((1, nc, tk, WB), lambda bi, i: (bi, 0, 0, 0)),
            pl.BlockSpec((1, DVB), lambda bi, i: (0, 0)),
        ],
        out_specs=pl.BlockSpec((1, tq, WB), lambda bi, i: (bi, i, 0)),
        out_shape=jax.ShapeDtypeStruct((b, s, WB), jnp.float32),
        scratch_shapes=[
            pltpu.VMEM((HB * 2, tq, LANES), jnp.float32),
            pltpu.VMEM((HB * 2, tq, LANES), jnp.float32),
            pltpu.VMEM((HB * 2, tq, DVB), jnp.float32),
        ],
        compiler_params=pltpu.CompilerParams(dimension_semantics=("arbitrary", "arbitrary"),
                                             vmem_limit_bytes=VMEM_LIMIT_BYTES),
        name="diff_prompt",
    )(lam.reshape(1).astype(jnp.float32), q.astype(bf), kt, vc, subln_g.reshape(1, DVB).astype(jnp.float32))


PAGES_PER_STEP = 16
SELECT_PAGES_PER_STEP = 32


def _order_key(x):
    x = jnp.where(x == 0.0, 0.0, x)
    bits = pltpu.bitcast(x, jnp.int32)
    return bits ^ ((bits >> 31) & 0x7FFFFFFF)


def _page_specs(block, layer, n_rep):
    def spec(g):
        return pl.BlockSpec((None, None) + block,
                            lambda b, j, pt: (layer, pt[b, j * n_rep + g], 0, 0))
    return [spec(g) for g in range(n_rep)]


def _dsa_sample_select_body(pt_ref, qi_ref, w_ref, kin_ref, *rest, pg, nj, t_new, topk, idx_bits):
    page_refs, (bias_ref, key_scr, keyn_scr) = rest[:pg], rest[pg:]
    j = pl.program_id(1)
    width = pg * PAGE_SIZE
    bf = jnp.bfloat16

    def scores(kit):
        qk = jnp.dot(qi_ref[0], kit.astype(bf), preferred_element_type=jnp.float32)
        sc = jnp.maximum(qk, 0.0) * w_ref[0]
        out = sc[0:t_new]
        for h in range(1, HI):
            out = out + sc[h * t_new:(h + 1) * t_new]
        return out

    key_scr[j] = _order_key(scores(jnp.concatenate([r[...] for r in page_refs], axis=1)))

    @pl.when(j == nj - 1)
    def _():
        past = nj * width
        tok = lax.broadcasted_iota(jnp.int32, (t_new, 1), 0)
        lane_n = lax.broadcasted_iota(jnp.int32, (1, PAGE_SIZE), 1)
        sc_n = jnp.where(lane_n <= tok, scores(kin_ref[0]), -jnp.inf)
        keyn_scr[...] = _order_key(sc_n)
        lane_p = lax.broadcasted_iota(jnp.int32, (1, width), 1)
        idx_n = past + lane_n

        def count(pred):
            part = jnp.zeros((t_new, width), jnp.int32)
            for s in range(nj):
                part = part + pred(key_scr[s], s * width + lane_p).astype(jnp.int32)
            return (jnp.sum(part, axis=1, keepdims=True)
                    + jnp.sum(pred(keyn_scr[...], idx_n).astype(jnp.int32), axis=1, keepdims=True))

        def thr_bit(bi, t):
            cand = t + jnp.left_shift(jnp.int32(1), 31 - bi)
            return jnp.where(count(lambda k, idx: k >= cand) >= topk, cand, t)

        thr = lax.fori_loop(0, 32, thr_bit, jnp.full((t_new, 1), INT32_MIN, jnp.int32))
        need = topk - count(lambda k, idx: k > thr)

        def tie_bit(bi, jb):
            cand = jb + jnp.left_shift(jnp.int32(1), idx_bits - 1 - bi)
            before = count(lambda k, idx: (k == thr) & (idx < cand))
            return jnp.where(before <= need - 1, cand, jb)

        tie = lax.fori_loop(0, idx_bits, tie_bit, jnp.zeros((t_new, 1), jnp.int32))

        def bias_of(k, idx):
            return jnp.where((k > thr) | ((k == thr) & (idx <= tie)), 0.0, MASK_BIAS)

        for s in range(nj):
            idx_s = s * width + lax.broadcasted_iota(jnp.int32, (1, width), 1)
            bias_ref[0, :, s * width:(s + 1) * width] = bias_of(key_scr[s], idx_s)
        bias_ref[0, :, past:past + PAGE_SIZE] = jnp.where(lane_n <= tok, bias_of(keyn_scr[...], idx_n), MASK_BIAS)


def _dsa_sample_attn_body(pt_ref, q_ref, bias_ref, biasn_ref, knt_ref, vnt_ref, *rest, pg, t_new):
    k_refs, v_refs = rest[:pg], rest[pg:2 * pg]
    o_ref, m_ref, l_ref, acc_ref = rest[2 * pg:]
    j = pl.program_id(1)
    bf = jnp.bfloat16

    @pl.when(j == 0)
    def _():
        m_ref[...] = jnp.full(m_ref.shape, M_INIT, jnp.float32)
        l_ref[...] = jnp.zeros(l_ref.shape, jnp.float32)
        acc_ref[...] = jnp.zeros(acc_ref.shape, jnp.float32)

    def update(kt, vt, bias):
        s = jnp.dot(q_ref[0], kt.astype(bf), preferred_element_type=jnp.float32)
        s = s + jnp.concatenate([bias] * HC, axis=0)
        m_prev = m_ref[...]
        m_new = jnp.maximum(m_prev, jnp.max(s, axis=1, keepdims=True))
        alpha = jnp.exp(m_prev - m_new)
        p = jnp.exp(s - m_new[:, :1])
        l_ref[...] = alpha * l_ref[...] + jnp.sum(p, axis=1, keepdims=True)
        pv = lax.dot_general(p.astype(bf), vt.astype(bf), (((1,), (1,)), ((), ())),
                             preferred_element_type=jnp.float32)
        acc_ref[...] = acc_ref[...] * jnp.concatenate([alpha] * (WC // LANES), axis=1) + pv
        m_ref[...] = m_new

    update(jnp.concatenate([r[...] for r in k_refs], axis=1),
           jnp.concatenate([r[...] for r in v_refs], axis=1), bias_ref[0])

    @pl.when(j == pl.num_programs(1) - 1)
    def _():
        update(knt_ref[0], vnt_ref[0], biasn_ref[0])
        head_of_lane = lax.broadcasted_iota(jnp.int32, (1, WC), 1) // DHC
        out = jnp.zeros((t_new, WC), jnp.float32)
        for h in range(HC):
            rows = slice(h * t_new, (h + 1) * t_new)
            o_h = acc_ref[rows] / jnp.concatenate([l_ref[rows]] * (WC // LANES), axis=1)
            out = jnp.where(head_of_lane == h, o_h, out)
        o_ref[0] = out


def dsa_sample_pallas(q, k_new, v_new, qi, wi, ki_new, pool_k, pool_v, pool_ki, page_table, layer,
                      *, pg=PAGES_PER_STEP, pg_select=SELECT_PAGES_PER_STEP):
    db, t_new, _ = q.shape
    n_pages = page_table.shape[1]
    past = n_pages * PAGE_SIZE
    assert n_pages % pg == 0 and n_pages % pg_select == 0 and t_new <= PAGE_SIZE
    nj, nj_s = n_pages // pg, n_pages // pg_select
    width, width_s = pg * PAGE_SIZE, pg_select * PAGE_SIZE
    total = past + PAGE_SIZE
    topk = min(TOPK_MAX, (past + t_new) // 4)
    bf = jnp.bfloat16
    n_layers, n_pool = pool_k.shape[:2]
    kit_pool = pool_ki.transpose(0, 1, 3, 2)
    kt_pool = pool_k.transpose(0, 1, 3, 4, 2).reshape(n_layers, n_pool, WC, PAGE_SIZE)
    vt_pool = pool_v.transpose(0, 1, 3, 4, 2).reshape(n_layers, n_pool, WC, PAGE_SIZE)
    pad_t = lambda x: jnp.pad(x.swapaxes(1, 2), ((0, 0), (0, 0), (0, PAGE_SIZE - t_new)))
    qi2 = qi.reshape(db, t_new, HI, DI).swapaxes(1, 2).reshape(db, HI * t_new, DI).astype(bf)
    w2 = wi.swapaxes(1, 2).reshape(db, HI * t_new, 1)
    seq_spec = lambda shape: pl.BlockSpec((1,) + shape, lambda b, j, pt: (b, 0, 0))
    bias = pl.pallas_call(
        functools.partial(_dsa_sample_select_body, pg=pg_select, nj=nj_s, t_new=t_new, topk=topk,
                          idx_bits=int(total).bit_length()),
        grid_spec=pltpu.PrefetchScalarGridSpec(
            num_scalar_prefetch=1, grid=(db, nj_s),
            in_specs=[seq_spec((HI * t_new, DI)), seq_spec((HI * t_new, 1)), seq_spec((DI, PAGE_SIZE))]
            + _page_specs((DI, PAGE_SIZE), layer, pg_select),
            out_specs=seq_spec((t_new, total)),
            scratch_shapes=[pltpu.VMEM((nj_s, t_new, width_s), jnp.int32),
                            pltpu.VMEM((t_new, PAGE_SIZE), jnp.int32)]),
        out_shape=jax.ShapeDtypeStruct((db, t_new, total), jnp.float32),
        compiler_params=pltpu.CompilerParams(dimension_semantics=("arbitrary", "arbitrary")),
        name="dsa_sample_select",
    )(page_table, qi2, w2, pad_t(ki_new).astype(bf), *([kit_pool] * pg_select))

    qh = q.reshape(db, t_new, HC, DHC).swapaxes(1, 2)
    qbd = (qh[:, :, :, None, :] * jnp.eye(HC, dtype=q.dtype)[None, :, None, :, None])
    qbd = qbd.reshape(db, HC * t_new, WC).astype(bf)
    return pl.pallas_call(
        functools.partial(_dsa_sample_attn_body, pg=pg, t_new=t_new),
        grid_spec=pltpu.PrefetchScalarGridSpec(
            num_scalar_prefetch=1, grid=(db, nj),
            in_specs=[seq_spec((HC * t_new, WC)),
                      pl.BlockSpec((1, t_new, width), lambda b, j, pt: (b, 0, j)),
                      pl.BlockSpec((1, t_new, PAGE_SIZE), lambda b, j, pt: (b, 0, n_pages)),
                      seq_spec((WC, PAGE_SIZE)), seq_spec((WC, PAGE_SIZE))]
            + _page_specs((WC, PAGE_SIZE), layer, pg) + _page_specs((WC, PAGE_SIZE), layer, pg),
            out_specs=seq_spec((t_new, WC)),
            scratch_shapes=[pltpu.VMEM((HC * t_new, LANES), jnp.float32),
                            pltpu.VMEM((HC * t_new, LANES), jnp.float32),
                            pltpu.VMEM((HC * t_new, WC), jnp.float32)]),
        out_shape=jax.ShapeDtypeStruct((db, t_new, WC), jnp.float32),
        compiler_params=pltpu.CompilerParams(dimension_semantics=("arbitrary", "arbitrary")),
        name="dsa_sample_attn",
    )(page_table, qbd, bias, bias, pad_t(k_new).astype(bf), pad_t(v_new).astype(bf),
      *([kt_pool] * pg), *([vt_pool] * pg))


def _diff_sample_body(pt_ref, lam_ref, q_ref, knt_ref, vn_ref, g_ref, *rest, pg, t_new, out_scale):
    k_refs, v_refs = rest[:pg], rest[pg:2 * pg]
    o_ref, m_ref, l_ref, acc_ref = rest[2 * pg:]
    j = pl.program_id(1)
    bf = jnp.bfloat16

    @pl.when(j == 0)
    def _():
        m_ref[...] = jnp.full(m_ref.shape, M_INIT, jnp.float32)
        l_ref[...] = jnp.zeros(l_ref.shape, jnp.float32)
        acc_ref[...] = jnp.zeros(acc_ref.shape, jnp.float32)

    def update(kt, v, bias):
        s = jnp.dot(q_ref[0], kt.astype(bf), preferred_element_type=jnp.float32)
        if bias is not None:
            s = s + bias
        m_prev = m_ref[...]
        m_new = jnp.maximum(m_prev, jnp.max(s, axis=1, keepdims=True))
        alpha = jnp.exp(m_prev - m_new)
        p = jnp.exp(s - m_new[:, :1])
        l_ref[...] = alpha * l_ref[...] + jnp.sum(p, axis=1, keepdims=True)
        pv = jnp.dot(p.astype(bf), v.astype(bf), preferred_element_type=jnp.float32)
        acc_ref[...] = acc_ref[...] * jnp.concatenate([alpha] * HB, axis=1) + pv
        m_ref[...] = m_new

    v_pages = [jnp.concatenate([r[pl.ds(h, PAGE_SIZE, stride=HB), :] for h in range(HB)], axis=1) for r in v_refs]
    update(jnp.concatenate([r[...] for r in k_refs], axis=1), jnp.concatenate(v_pages, axis=0), None)

    @pl.when(j == pl.num_programs(1) - 1)
    def _():
        n_rows = 2 * HB * t_new
        tok = lax.broadcasted_iota(jnp.int32, (n_rows, 1), 0) % t_new
        lane_n = lax.broadcasted_iota(jnp.int32, (1, PAGE_SIZE), 1)
        update(knt_ref[0], vn_ref[0], jnp.where(lane_n <= tok, 0.0, MASK_BIAS))
        lam = lam_ref[0]
        outs = []
        for h in range(HB):
            r1 = slice(2 * h * t_new, (2 * h + 1) * t_new)
            r2 = slice((2 * h + 1) * t_new, (2 * h + 2) * t_new)
            cols = slice(h * DVB, (h + 1) * DVB)
            o = acc_ref[r1, cols] / l_ref[r1] - lam * (acc_ref[r2, cols] / l_ref[r2])
            o = o * lax.rsqrt(jnp.mean(o * o, axis=1, keepdims=True) + DIFF_SUBLN_EPS)
            outs.append(o * g_ref[...] * out_scale)
        o_ref[0] = jnp.concatenate(outs, axis=1)


def diff_sample_pallas(q, k_new, v_new, pool_k, pool_v, page_table, layer, lam, subln_g, out_scale,
                       *, pg=PAGES_PER_STEP):
    db, t_new, wq = q.shape
    n_pages = page_table.shape[1]
    assert n_pages % pg == 0 and t_new <= PAGE_SIZE and DVB == LANES
    nj = n_pages // pg
    bf = jnp.bfloat16
    n_layers, n_pool = pool_k.shape[:2]
    kt_pool = pool_k.transpose(0, 1, 3, 4, 5, 2).reshape(n_layers, n_pool, wq, PAGE_SIZE)
    v_pool = pool_v.reshape(n_layers, n_pool, PAGE_SIZE * HB, DVB)
    n_hc = 2 * HB
    qh = q.reshape(db, t_new, n_hc, DHB).swapaxes(1, 2)
    qbd = (qh[:, :, :, None, :] * jnp.eye(n_hc, dtype=q.dtype)[None, :, None, :, None])
    qbd = qbd.reshape(db, n_hc * t_new, wq).astype(bf)
    knt = jnp.pad(k_new.swapaxes(1, 2), ((0, 0), (0, 0), (0, PAGE_SIZE - t_new))).astype(bf)
    vn = jnp.pad(v_new, ((0, 0), (0, PAGE_SIZE - t_new), (0, 0))).astype(bf)
    seq_spec = lambda shape: pl.BlockSpec((1,) + shape, lambda b, j, pt: (b, 0, 0))
    return pl.pallas_call(
        functools.partial(_diff_sample_body, pg=pg, t_new=t_new, out_scale=out_scale),
        grid_spec=pltpu.PrefetchScalarGridSpec(
            num_scalar_prefetch=1, grid=(db, nj),
            in_specs=[pl.BlockSpec(memory_space=pltpu.SMEM),
                      seq_spec((n_hc * t_new, wq)), seq_spec((wq, PAGE_SIZE)), seq_spec((PAGE_SIZE, WB)),
                      pl.BlockSpec((1, DVB), lambda b, j, pt: (0, 0))]
            + _page_specs((wq, PAGE_SIZE), layer, pg) + _page_specs((PAGE_SIZE * HB, DVB), layer, pg),
            out_specs=seq_spec((t_new, WB)),
            scratch_shapes=[pltpu.VMEM((n_hc * t_new, LANES), jnp.float32),
                            pltpu.VMEM((n_hc * t_new, LANES), jnp.float32),
                            pltpu.VMEM((n_hc * t_new, WB), jnp.float32)]),
        out_shape=jax.ShapeDtypeStruct((db, t_new, WB), jnp.float32),
        compiler_params=pltpu.CompilerParams(dimension_semantics=("arbitrary", "arbitrary"),
                                             vmem_limit_bytes=VMEM_LIMIT_BYTES),
        name="diff_sample",
    )(page_table, lam.reshape(1).astype(jnp.float32), qbd, knt, vn,
      subln_g.reshape(1, DVB).astype(jnp.float32), *([kt_pool] * pg), *([v_pool] * pg))


RWKV_TBLK = LANES // 2
RWKV_PAIRS = HA // 2


def _rwkv_scan_body(r_ref, w_ref, k_ref, kk_ref, b_ref, vt_ref, e_ref, s0_ref, ot_ref, sout_ref,
                    s_scr, vb_scr, sr_scr, *, bb, nsteps):
    ti = pl.program_id(1)

    @pl.when(ti == 0)
    def _():
        s_scr[...] = s0_ref[...]

    slot = lax.broadcasted_iota(jnp.int32, (1, LANES), 1) & (RWKV_TBLK - 1)
    e = e_ref[...]
    bf = jnp.bfloat16
    units = [(bi, p) for bi in range(bb) for p in range(RWKV_PAIRS)]

    for u, (bi, p) in enumerate(units):
        vt = vt_ref[bi, p, 0]
        lhs = jnp.concatenate([jnp.where(slot == t, vt, 0.0).astype(bf) for t in range(nsteps)], axis=0)
        vb_scr[u] = jnp.dot(lhs, e, preferred_element_type=jnp.float32).reshape(nsteps, NA, LANES)

    def step(t, carry):
        rows = [[ref[bi, pl.ds(t, 1), :] for ref in (r_ref, w_ref, k_ref, kk_ref, b_ref)] for bi in range(bb)]
        ms = [(s_scr[bi, p] * rows[bi][3][:, p * LANES:(p + 1) * LANES]).astype(bf) for bi, p in units]
        sk_all = jnp.dot(jnp.concatenate(ms, axis=0), e, preferred_element_type=jnp.float32)
        for u, (bi, p) in enumerate(units):
            cols = slice(p * LANES, (p + 1) * LANES)
            r_t, w_t, k_t, _, b_t = rows[bi]
            sk = sk_all[u * NA:(u + 1) * NA]
            s = s_scr[bi, p] * w_t[:, cols] - sk * b_t[:, cols] + vb_scr[u, t] * k_t[:, cols]
            s_scr[bi, p] = s
            sr_scr[u, t] = s * r_t[:, cols]
        return carry

    lax.fori_loop(0, nsteps, step, 0)

    for u, (bi, p) in enumerate(units):
        o_b = jnp.dot(sr_scr[u].reshape(nsteps * NA, LANES).astype(bf), e, preferred_element_type=jnp.float32)
        ot = jnp.zeros((NA, LANES), jnp.float32)
        for t in range(nsteps):
            ot = jnp.where(slot == t, o_b[t * NA:(t + 1) * NA], ot)
        ot_ref[bi, p, 0] = ot

    @pl.when(ti == pl.num_programs(1) - 1)
    def _():
        sout_ref[...] = s_scr[...]


def rwkv_scan_pallas(r, w, k, v, kk, b, s0, *, bb=2):
    nb, t, _ = r.shape
    nsteps = min(RWKV_TBLK, t)
    assert nb % bb == 0 and t % nsteps == 0 and LANES == 2 * NA
    nblk = t // nsteps
    vt = v.reshape(nb, nblk, nsteps, RWKV_PAIRS, 2, NA).transpose(0, 3, 1, 5, 4, 2)
    vt = jnp.pad(vt, ((0, 0),) * 5 + ((0, RWKV_TBLK - nsteps),)).reshape(nb, RWKV_PAIRS, nblk, NA, LANES)
    s0p = s0.reshape(nb, RWKV_PAIRS, 2, NA, NA).transpose(0, 1, 3, 2, 4).reshape(nb, RWKV_PAIRS, NA, LANES)
    head_of_lane = jnp.arange(LANES) // NA
    e = (head_of_lane[:, None] == head_of_lane[None, :]).astype(jnp.bfloat16)
    row_spec = pl.BlockSpec((bb, nsteps, WA), lambda bi, ti: (bi, ti, 0))
    col_spec = pl.BlockSpec((bb, RWKV_PAIRS, 1, NA, LANES), lambda bi, ti: (bi, 0, ti, 0, 0))
    st_spec = pl.BlockSpec((bb, RWKV_PAIRS, NA, LANES), lambda bi, ti: (bi, 0, 0, 0))
    ot, s_fin = pl.pallas_call(
        functools.partial(_rwkv_scan_body, bb=bb, nsteps=nsteps),
        grid=(nb // bb, nblk),
        in_specs=[row_spec, row_spec, row_spec, row_spec, row_spec, col_spec,
                  pl.BlockSpec((LANES, LANES), lambda bi, ti: (0, 0)), st_spec],
        out_specs=[col_spec, st_spec],
        out_shape=[jax.ShapeDtypeStruct((nb, RWKV_PAIRS, nblk, NA, LANES), jnp.float32),
                   jax.ShapeDtypeStruct((nb, RWKV_PAIRS, NA, LANES), jnp.float32)],
        scratch_shapes=[pltpu.VMEM((bb, RWKV_PAIRS, NA, LANES), jnp.float32),
                        pltpu.VMEM((bb * RWKV_PAIRS, nsteps, NA, LANES), jnp.float32),
                        pltpu.VMEM((bb * RWKV_PAIRS, nsteps, NA, LANES), jnp.float32)],
        compiler_params=pltpu.CompilerParams(dimension_semantics=("arbitrary", "arbitrary"),
                                             vmem_limit_bytes=VMEM_LIMIT_BYTES),
        name="rwkv_scan",
    )(r, w, k, kk, b, vt, e, s0p)
    o = ot.reshape(nb, RWKV_PAIRS, nblk, NA, 2, RWKV_TBLK)[..., :nsteps]
    o = o.transpose(0, 2, 5, 1, 4, 3).reshape(nb, t, WA)
    s_fin = s_fin.reshape(nb, RWKV_PAIRS, NA, 2, NA).transpose(0, 1, 3, 2, 4).reshape(nb, HA, NA, NA)
    return o, s_fin


ROW_TILE = 256
N_MAIN = NCA + NCB + NCC
N_MAIN_PAD = -(-N_MAIN // LANES) * LANES
PROJ_SEGMENTS = (
    ("pa", 0, NCA, jnp.float32, 1.0),
    ("qb", NCA, HB * 2 * DHB, jnp.bfloat16, DHB ** -0.5),
    ("kb", NCA + HB * 2 * DHB, HB * 2 * DHB, jnp.float32, 1.0),
    ("vb", NCA + 2 * HB * 2 * DHB, WB, jnp.float32, 1.0),
    ("qc", NCA + NCB, WC, jnp.bfloat16, DHC ** -0.5),
    ("kc", NCA + NCB + WC, WC, jnp.float32, 1.0),
    ("vc", NCA + NCB + 2 * WC, WC, jnp.float32, 1.0),
    ("qi", NCA + NCB + 3 * WC, HI * DI, jnp.bfloat16, DI ** -0.5),
)
KI_START = NCA + NCB + 3 * WC + HI * DI


def _proj_body(x_ref, g_ref, wm_ref, wg_ref, *out_refs):
    seg_refs, (ki_ref, wi_ref, gate_ref) = out_refs[:len(PROJ_SEGMENTS)], out_refs[len(PROJ_SEGMENTS):]
    x = x_ref[...]
    xn = (x * lax.rsqrt(jnp.mean(x * x, axis=-1, keepdims=True) + NORM_EPS) * g_ref[...]).astype(jnp.bfloat16)
    for (name, start, width, dtype, scale), ref in zip(PROJ_SEGMENTS, seg_refs):
        y = jnp.dot(xn, wm_ref[:, start:start + width], preferred_element_type=jnp.float32)
        ref[...] = (y * scale if scale != 1.0 else y).astype(dtype)
    tail = jnp.dot(xn, wm_ref[:, KI_START:KI_START + LANES], preferred_element_type=jnp.float32)
    ki_ref[...] = tail[:, :DI]
    wi_ref[...] = tail[:, DI:DI + HI] * (HI ** -0.5)
    for c in range(N_BRANCH):
        y = jnp.dot(xn, wg_ref[:, c * D_MODEL:(c + 1) * D_MODEL], preferred_element_type=jnp.float32)
        gate_ref[:, c * D_MODEL:(c + 1) * D_MODEL] = jax.nn.sigmoid(y)


def proj_pallas(x2d, g, w_in_l):
    t, d = x2d.shape
    tm = min(ROW_TILE, t)
    assert t % tm == 0 and KI_START % LANES == 0 and KI_START + DI + HI == N_MAIN
    bf = jnp.bfloat16
    w_main = jnp.pad(w_in_l[:, :N_MAIN], ((0, 0), (0, N_MAIN_PAD - N_MAIN))).astype(bf)
    w_gate = w_in_l[:, N_MAIN:].astype(bf)
    row = lambda width: pl.BlockSpec((tm, width), lambda i: (i, 0))
    full = lambda a: pl.BlockSpec(a.shape, lambda i: (0, 0))
    widths = [s[2] for s in PROJ_SEGMENTS] + [DI, HI, N_BRANCH * D_MODEL]
    dtypes = [s[3] for s in PROJ_SEGMENTS] + [jnp.float32] * 3
    outs = pl.pallas_call(
        _proj_body,
        grid=(t // tm,),
        in_specs=[row(d), pl.BlockSpec((1, d), lambda i: (0, 0)), full(w_main), full(w_gate)],
        out_specs=[row(w) for w in widths],
        out_shape=[jax.ShapeDtypeStruct((t, w), dt) for w, dt in zip(widths, dtypes)],
        compiler_params=pltpu.CompilerParams(dimension_semantics=("arbitrary",),
                                             vmem_limit_bytes=VMEM_LIMIT_BYTES),
        name="in_proj",
    )(x2d, g.reshape(1, d), w_main, w_gate)
    names = [s[0] for s in PROJ_SEGMENTS] + ["ki", "wi", "gates"]
    return dict(zip(names, outs))


def _head_sum(x, e):
    x_hi = x.astype(jnp.bfloat16)
    x_lo = (x - x_hi.astype(jnp.float32)).astype(jnp.bfloat16)
    return (jnp.dot(x_hi, e, preferred_element_type=jnp.float32)
            + jnp.dot(x_lo, e, preferred_element_type=jnp.float32))


def _head_ones():
    head_of_lane = jnp.arange(WA) // NA
    return (head_of_lane[:, None] == head_of_lane[None, :]).astype(jnp.bfloat16)


def _rwkv_pre_body(p_ref, ps_ref, mu_ref, w0_ref, wup_ref, a0_ref, aup_ref, gup_ref, kk_ref, ka_ref, rk_ref,
                   e_ref, r_o, w_o, k_o, v_o, kk_o, b_o, g_o, bonus_o):
    bf = jnp.bfloat16
    e = e_ref[...]
    p = p_ref[...]
    px = p + (ps_ref[...] - p) * mu_ref[...]
    r, k, v = px[:, 0:WA], px[:, WA:2 * WA], px[:, 2 * WA:3 * WA]
    o = 3 * WA
    wd, ad, gd = px[:, o:o + LORA_W], px[:, o + LORA_W:o + LORA_W + LORA_A], px[:, o + LORA_W + LORA_A:]
    dot = lambda x, w_ref: jnp.dot(x.astype(bf), w_ref[...], preferred_element_type=jnp.float32)
    z = -(w0_ref[...] + dot(jnp.tanh(wd), wup_ref))
    softplus = jnp.maximum(z, 0.0) + jnp.log(1.0 + jnp.exp(-jnp.abs(z)))
    decay = jnp.exp(-jnp.exp(-softplus - 0.5))
    a = jax.nn.sigmoid(a0_ref[...] + dot(ad, aup_ref))
    kk = k * kk_ref[...]
    kk = kk * lax.rsqrt(jnp.maximum(_head_sum(kk * kk, e), 1e-24))
    k = k * (1.0 + (a - 1.0) * ka_ref[...])
    r_o[...] = r
    w_o[...] = decay
    k_o[...] = k
    v_o[...] = v
    kk_o[...] = kk
    b_o[...] = kk * a
    g_o[...] = dot(jax.nn.sigmoid(gd), gup_ref)
    bonus_o[...] = _head_sum(r * k * rk_ref[...], e) * v


def rwkv_pre_pallas(pa2d, ps2d, mu, w0, w_up, a0, a_up, g_up, k_k, k_a, r_k):
    t, _ = pa2d.shape
    tm = min(ROW_TILE, t)
    bf = jnp.bfloat16
    vecs = [x.reshape(1, -1).astype(jnp.float32) for x in (mu, w0, a0, k_k, k_a, r_k)]
    mats = [x.astype(bf) for x in (w_up, a_up, g_up)]
    row = lambda width: pl.BlockSpec((tm, width), lambda i: (i, 0))
    full = lambda a: pl.BlockSpec(a.shape, lambda i: (0, 0))
    args = [pa2d, ps2d, vecs[0], vecs[1], mats[0], vecs[2], mats[1], mats[2], vecs[3], vecs[4], vecs[5], _head_ones()]
    return pl.pallas_call(
        _rwkv_pre_body,
        grid=(t // tm,),
        in_specs=[row(NCA), row(NCA)] + [full(a) for a in args[2:]],
        out_specs=[row(WA)] * 8,
        out_shape=[jax.ShapeDtypeStruct((t, WA), jnp.float32)] * 8,
        compiler_params=pltpu.CompilerParams(dimension_semantics=("arbitrary",)),
        name="rwkv_pre",
    )(*args)


def _merge_body(h_ref, o_ref, bonus_ref, g_ref, lnw_ref, lnb_ref, e_ref, ob_ref, oc_ref, gate_ref,
                wa_ref, wb_ref, wc_ref, wo_ref, out_ref):
    bf = jnp.bfloat16
    e = e_ref[...]
    o = o_ref[...]
    c = o - _head_sum(o, e) * (1.0 / NA)
    var = _head_sum(c * c, e) * (1.0 / NA)
    oa = (c * lax.rsqrt(var + RWKV_GN_EPS) * lnw_ref[...] + lnb_ref[...] + bonus_ref[...]) * g_ref[...]
    merged = None
    for ci, (x, w_ref) in enumerate(((oa, wa_ref), (ob_ref[...], wb_ref), (oc_ref[...], wc_ref))):
        br = jnp.dot(x.astype(bf), w_ref[...], preferred_element_type=jnp.float32)
        term = gate_ref[:, ci * D_MODEL:(ci + 1) * D_MODEL] * br
        merged = term if merged is None else merged + term
    out_ref[...] = h_ref[...] + jnp.dot(merged.astype(bf), wo_ref[...], preferred_element_type=jnp.float32)


def merge_pallas(h2d, o_rwkv, bonus, g_rwkv, ln_w, ln_b, ob, oc, gates, w_a, w_b, w_c, w_o):
    t, d = h2d.shape
    tm = min(ROW_TILE, t)
    bf = jnp.bfloat16
    ws = [w.astype(bf) for w in (w_a, w_b, w_c, w_o)]
    row = lambda a: pl.BlockSpec((tm, a.shape[1]), lambda i: (i, 0))
    full = lambda a: pl.BlockSpec(a.shape, lambda i: (0, 0))
    lnw, lnb, e = ln_w.reshape(1, WA), ln_b.reshape(1, WA), _head_ones()
    return pl.pallas_call(
        _merge_body,
        grid=(t // tm,),
        in_specs=[row(a) for a in (h2d, o_rwkv, bonus, g_rwkv)] + [full(a) for a in (lnw, lnb, e)]
        + [row(a) for a in (ob, oc, gates)] + [full(w) for w in ws],
        out_specs=row(h2d),
        out_shape=jax.ShapeDtypeStruct((t, d), jnp.float32),
        compiler_params=pltpu.CompilerParams(dimension_semantics=("arbitrary",),
                                             vmem_limit_bytes=VMEM_LIMIT_BYTES),
        name="branch_merge",
    )(h2d, o_rwkv, bonus, g_rwkv, lnw, lnb, e, ob, oc, gates, *ws)


FFN_CHUNK = 256


def _ffn_body(h_ref, g_ref, wg_ref, wu_ref, wd_ref, gf_ref, o_ref, acc_ref, *, final_norm):
    bf = jnp.bfloat16
    h = h_ref[...]
    xn = (h * lax.rsqrt(jnp.mean(h * h, axis=-1, keepdims=True) + NORM_EPS) * g_ref[...]).astype(bf)
    acc_ref[...] = h
    for f in range(0, D_FF, FFN_CHUNK):
        gate = jnp.dot(xn, wg_ref[:, f:f + FFN_CHUNK], preferred_element_type=jnp.float32)
        up = jnp.dot(xn, wu_ref[:, f:f + FFN_CHUNK], preferred_element_type=jnp.float32)
        act = (gate * jax.nn.sigmoid(gate) * up).astype(bf)
        acc_ref[...] += jnp.dot(act, wd_ref[f:f + FFN_CHUNK, :], preferred_element_type=jnp.float32)
    y = acc_ref[...]
    if final_norm:
        y = y * lax.rsqrt(jnp.mean(y * y, axis=-1, keepdims=True) + NORM_EPS) * gf_ref[...]
    o_ref[...] = y


def ffn_pallas(h2d, g, w_gate, w_up, w_down, g_final, *, final_norm):
    t, d = h2d.shape
    tm = min(ROW_TILE, t)
    assert D_FF % FFN_CHUNK == 0
    bf = jnp.bfloat16
    ws = [w.astype(bf) for w in (w_gate, w_up, w_down)]
    row = pl.BlockSpec((tm, d), lambda i: (i, 0))
    vec = pl.BlockSpec((1, d), lambda i: (0, 0))
    full = lambda a: pl.BlockSpec(a.shape, lambda i: (0, 0))
    return pl.pallas_call(
        functools.partial(_ffn_body, final_norm=final_norm),
        grid=(t // tm,),
        in_specs=[row, vec] + [full(w) for w in ws] + [vec],
        out_specs=row,
        out_shape=jax.ShapeDtypeStruct((t, d), jnp.float32),
        scratch_shapes=[pltpu.VMEM((tm, d), jnp.float32)],
        compiler_params=pltpu.CompilerParams(dimension_semantics=("arbitrary",),
                                             vmem_limit_bytes=VMEM_LIMIT_BYTES),
        name="ffn",
    )(h2d, g.reshape(1, d), *ws, g_final.reshape(1, d))


def _layer_group(h, shift_prev, state0, layer, attend, p):
    b, t, d = h.shape
    n = b * t
    pr = proj_pallas(h.reshape(n, d), p["g_mix"][layer], p["w_in"][layer])
    pa = pr["pa"].reshape(b, t, NCA)
    pa_shift = jnp.concatenate([shift_prev[:, None], pa[:, :-1]], axis=1)
    r, decay, k, v, kk, kk_a, g_rwkv, bonus = rwkv_pre_pallas(
        pr["pa"], pa_shift.reshape(n, NCA), p["rwkv_mu"][layer], p["rwkv_w0"][layer], p["rwkv_w_up"][layer],
        p["rwkv_a0"][layer], p["rwkv_a_up"][layer], p["rwkv_g_up"][layer], p["rwkv_k_k"][layer],
        p["rwkv_k_a"][layer], p["rwkv_r_k"][layer])
    seq = lambda x: x.reshape(b, t, WA)
    o_rwkv, s_fin = rwkv_scan_pallas(seq(r), seq(decay), seq(k), seq(v), seq(kk), seq(kk_a), state0)
    ob, oc = attend(pr)
    h2 = merge_pallas(h.reshape(n, d), o_rwkv.reshape(n, WA), bonus, g_rwkv, p["rwkv_ln_w"][layer],
                      p["rwkv_ln_b"][layer], ob, oc, pr["gates"], p["w_br_a"][layer], p["w_br_b"][layer],
                      p["w_br_c"][layer], p["w_out"][layer])
    h3 = ffn_pallas(h2, p["g_ffn"][layer], p["w_ffn_gate"][layer], p["w_ffn_up"][layer], p["w_ffn_down"][layer],
                    p["g_final"], final_norm=(layer == DEPTH - 1))
    caches = (s_fin, pa[:, -1], pr["kb"].reshape(b, t, HB, 2, DHB), pr["vb"].reshape(b, t, HB, DVB),
              pr["kc"].reshape(b, t, HC, DHC), pr["vc"].reshape(b, t, HC, DHC), pr["ki"].reshape(b, t, DI))
    return h3.reshape(b, t, d), caches


def kernel(x_prompt, x_sample, state_rwkv, state_shift, cache_diff_k, cache_diff_v, cache_dsa_k,
           cache_dsa_v, cache_idx_k, page_table, g_mix, w_in, rwkv_mu, rwkv_w0, rwkv_w_up, rwkv_a0,
           rwkv_a_up, rwkv_g_up, rwkv_k_k, rwkv_k_a, rwkv_r_k, rwkv_ln_w, rwkv_ln_b, diff_lam_q1,
           diff_lam_k1, diff_lam_q2, diff_lam_k2, diff_subln_g, w_br_a, w_br_b, w_br_c, w_out, g_ffn,
           w_ffn_gate, w_ffn_up, w_ffn_down, g_final):
    p = dict(g_mix=g_mix, w_in=w_in, rwkv_mu=rwkv_mu, rwkv_w0=rwkv_w0, rwkv_w_up=rwkv_w_up, rwkv_a0=rwkv_a0,
             rwkv_a_up=rwkv_a_up, rwkv_g_up=rwkv_g_up, rwkv_k_k=rwkv_k_k, rwkv_k_a=rwkv_k_a, rwkv_r_k=rwkv_r_k,
             rwkv_ln_w=rwkv_ln_w, rwkv_ln_b=rwkv_ln_b, w_br_a=w_br_a, w_br_b=w_br_b, w_br_c=w_br_c, w_out=w_out,
             g_ffn=g_ffn, w_ffn_gate=w_ffn_gate, w_ffn_up=w_ffn_up, w_ffn_down=w_ffn_down, g_final=g_final)
    bp, tp = x_prompt.shape[:2]
    bs, ts = x_sample.shape[:2]
    newp, news = [], []
    hp, hs = x_prompt, x_sample
    for l in range(DEPTH):
        lam_init = 0.8 - 0.6 * math.exp(-0.3 * l)
        lam = (jnp.exp(jnp.sum(diff_lam_q1[l] * diff_lam_k1[l])) - jnp.exp(jnp.sum(diff_lam_q2[l] * diff_lam_k2[l]))
               + lam_init).astype(jnp.float32)

        def attend_prompt(pr):
            seq = lambda x: x.reshape(bp, tp, -1)
            ob = diff_prompt_pallas(seq(pr["qb"]), seq(pr["kb"]), seq(pr["vb"]), lam, diff_subln_g[l], 1.0 - lam_init)
            oc = dsa_prompt_pallas(seq(pr["qc"]), seq(pr["kc"]), seq(pr["vc"]), seq(pr["qi"]), seq(pr["wi"]),
                                   seq(pr["ki"]))
            return ob.reshape(bp * tp, WB), oc.reshape(bp * tp, WC)

        def attend_sample(pr):
            seq = lambda x: x.reshape(bs, ts, -1)
            ob = diff_sample_pallas(seq(pr["qb"]), seq(pr["kb"]), seq(pr["vb"]), cache_diff_k, cache_diff_v,
                                    page_table, l, lam, diff_subln_g[l], 1.0 - lam_init)
            oc = dsa_sample_pallas(seq(pr["qc"]), seq(pr["kc"]), seq(pr["vc"]), seq(pr["qi"]), seq(pr["wi"]),
                                   seq(pr["ki"]), cache_dsa_k, cache_dsa_v, cache_idx_k, page_table, l)
            return ob.reshape(bs * ts, WB), oc.reshape(bs * ts, WC)

        hp, cp = _layer_group(hp, jnp.zeros((bp, NCA), jnp.float32), jnp.zeros((bp, HA, NA, NA), jnp.float32),
                              l, attend_prompt, p)
        newp.append(cp)
        hs, cs = _layer_group(hs, state_shift[l], state_rwkv[l], l, attend_sample, p)
        news.append(cs)

    stack = lambda group: tuple(jnp.stack([layer[i] for layer in group]) for i in range(7))
    return (hp, hs) + stack(newp) + stack(news)
```

```python
import functools
import math
import jax
import jax.numpy as jnp
from jax import lax
from jax.experimental import pallas as pl
from jax.experimental.pallas import tpu as pltpu

D_MODEL = 1024
DEPTH = 2
PAGE_SIZE = 128

HA = 4
NA = 64
WA = HA * NA
LORA_W = 64
LORA_A = 64
LORA_G = 128
NCA = 3 * WA + LORA_W + LORA_A + LORA_G
RWKV_GN_EPS = 64e-5
HB = 4
DHB = 64
DVB = 2 * DHB
WB = HB * DVB
NCB = 2 * (HB * 2 * DHB) + WB
DIFF_SUBLN_EPS = 1e-5
HC = 4
DHC = 64
WC = HC * DHC
HI = 8
DI = 64
TOPK_MAX = 256
NCC = 3 * WC + HI * DI + DI + HI
N_BRANCH = 3
N_IN = NCA + NCB + NCC + N_BRANCH * D_MODEL
D_FF = -(-8 * D_MODEL // (3 * 256)) * 256
NORM_EPS = 1e-6

VMEM_LIMIT_BYTES = 56 * 1024 * 1024
LANES = 128
MASK_BIAS = -1e30
M_INIT = -0.5e30
INT32_MIN = -(2 ** 31)


def _flash_update(s, v_bf16, m_ref, l_ref, acc_ref, slot, dv):
    m_prev = m_ref[slot]
    m_new = jnp.maximum(m_prev, jnp.max(s, axis=1, keepdims=True))
    alpha = jnp.exp(m_prev - m_new)
    p = jnp.exp(s - m_new[:, :1])
    l_ref[slot] = alpha * l_ref[slot] + jnp.sum(p, axis=1, keepdims=True)
    pv = jnp.dot(p.astype(jnp.bfloat16), v_bf16, preferred_element_type=jnp.float32)
    acc_ref[slot] = acc_ref[slot] * alpha[:, :dv] + pv
    m_ref[slot] = m_new


def _dsa_prompt_body(qi_ref, wi_ref, q_ref, kit_ref, kt_ref, v_ref, o_ref,
                     key_ref, keyhi_ref, keylo_ref, wb_ref, thr_ref, tie_ref, m_ref, l_ref, acc_ref, *, tq, tk, att_group, topk,
                     idx_bits):
    i = pl.program_id(1)
    nkc = (i + 1) * (tq // tk)
    qpos = i * tq + lax.broadcasted_iota(jnp.int32, (tq, 1), 0)

    for h in range(HI):
        wb_ref[h] = jnp.broadcast_to(wi_ref[0, :, h:h + 1], (tq, LANES))

    def score_body(c, carry):
        kit = kit_ref[0, c]
        tiles = [jnp.zeros((tq, LANES), jnp.float32)] * (tk // LANES)
        for h in range(HI):
            qk = jnp.dot(qi_ref[0, :, h * DI:(h + 1) * DI], kit, preferred_element_type=jnp.float32)
            w_b = wb_ref[h]
            tiles = [a + w_b * jnp.maximum(qk[:, j * LANES:(j + 1) * LANES], 0.0) for j, a in enumerate(tiles)]
        acc = jnp.concatenate(tiles, axis=1)
        kpos = c * tk + lax.broadcasted_iota(jnp.int32, (1, tk), 1)
        key = _order_key(jnp.where(kpos <= qpos, acc, -jnp.inf))
        key_ref[c] = key
        key_t = jnp.transpose(key)
        keyhi_ref[c] = (key_t >> 16).astype(jnp.int16)
        keylo_ref[c] = ((key_t & 0xFFFF) - 2 ** 15).astype(jnp.int16)
        return carry

    lax.fori_loop(0, nkc, score_body, 0)

    rows16 = 16
    i16 = jnp.int16

    def count(pred):
        def cbody(c, part):
            idx = (c * tk + lax.broadcasted_iota(jnp.int32, (tk, 1), 0)).astype(i16)
            hit = pred(keyhi_ref[c], keylo_ref[c], idx).astype(i16)
            for g in range(tk // rows16):
                part = part + hit[g * rows16:(g + 1) * rows16]
            return part
        part = lax.fori_loop(0, nkc, cbody, jnp.zeros((rows16, tq), i16))
        return jnp.sum(part.astype(jnp.int32), axis=0, keepdims=True)

    def search16(count_ge, target, n_bits=16):
        def bit(bi, t):
            cand = t + jnp.left_shift(jnp.int32(1), n_bits - 1 - bi)
            return jnp.where(count_ge(cand.astype(i16)) >= target, cand, t)
        return lax.fori_loop(0, n_bits, bit, jnp.full((1, tq), -(2 ** (n_bits - 1)), jnp.int32))

    thr_hi = search16(lambda v: count(lambda hi, lo, idx: hi >= v), topk)
    hi16 = thr_hi.astype(i16)
    n_above = count(lambda hi, lo, idx: hi > hi16)
    thr_lo = search16(lambda v: count(lambda hi, lo, idx: (hi == hi16) & (lo >= v)), topk - n_above)
    lo16 = thr_lo.astype(i16)
    thr = jnp.left_shift(thr_hi, 16) | ((thr_lo + 2 ** 15) & 0xFFFF)
    n_gt = n_above + count(lambda hi, lo, idx: (hi == hi16) & (lo > lo16))
    n_ge = n_above + count(lambda hi, lo, idx: (hi == hi16) & (lo >= lo16))
    need = topk - n_gt

    def tie_search():
        def tie_bit(bi, jbound):
            cand = jbound + jnp.left_shift(jnp.int32(1), idx_bits - 1 - bi)
            c16 = cand.astype(i16)
            before = count(lambda hi, lo, idx: (hi == hi16) & (lo == lo16) & (idx < c16))
            return jnp.where(before <= need - 1, cand, jbound)
        return lax.fori_loop(0, idx_bits, tie_bit, jnp.zeros((1, tq), jnp.int32))

    tie = lax.cond(jnp.max(n_ge) > topk, tie_search, lambda: jnp.full((1, tq), 2 ** 31 - 1, jnp.int32))
    thr_ref[...] = jnp.transpose(jnp.broadcast_to(thr, (LANES, tq)))
    tie_ref[...] = jnp.transpose(jnp.broadcast_to(tie, (LANES, tq)))

    m_ref[...] = jnp.full(m_ref.shape, M_INIT, jnp.float32)
    l_ref[...] = jnp.zeros(l_ref.shape, jnp.float32)
    acc_ref[...] = jnp.zeros(acc_ref.shape, jnp.float32)
    thr_c = thr_ref[:, :1]
    tie_c = tie_ref[:, :1]

    def attend(c0, n):
        kc = jnp.concatenate([key_ref[c0 + u] for u in range(n)], axis=1)
        idx = c0 * tk + lax.broadcasted_iota(jnp.int32, (1, n * tk), 1)
        sel = (kc > thr_c) | ((kc == thr_c) & (idx <= tie_c))
        bias = jnp.where(sel & (idx <= qpos), 0.0, MASK_BIAS)
        kt = jnp.concatenate([kt_ref[0, c0 + u] for u in range(n)], axis=1)
        vv = jnp.concatenate([v_ref[0, c0 + u] for u in range(n)], axis=0)
        for h in range(HC):
            s = jnp.dot(q_ref[0, :, h * DHC:(h + 1) * DHC], kt[h * DHC:(h + 1) * DHC, :],
                        preferred_element_type=jnp.float32) + bias
            _flash_update(s, vv[:, h * DHC:(h + 1) * DHC], m_ref, l_ref, acc_ref, h, DHC)

    def group_body(g, carry):
        attend(g * att_group, att_group)
        return carry

    n_groups = nkc // att_group
    lax.fori_loop(0, n_groups, group_body, 0)
    for rem in range(1, att_group):
        @pl.when(nkc - n_groups * att_group == rem)
        def _():
            attend(n_groups * att_group, rem)
    o_ref[0] = jnp.concatenate([acc_ref[h] / l_ref[h][:, :DHC] for h in range(HC)], axis=1)


def dsa_prompt_pallas(q, k, v, qi, wi, ki, *, tq=256, tk=256, att_group=4):
    b, s, _ = q.shape
    topk = min(TOPK_MAX, s // 4)
    assert s % tq == 0 and tq % tk == 0 and tq >= topk and s < 2 ** 15
    nc = s // tk
    bf = jnp.bfloat16
    kit = ki.astype(bf).reshape(b, nc, tk, DI).swapaxes(2, 3)
    kt = k.astype(bf).reshape(b, nc, tk, WC).swapaxes(2, 3)
    vc = v.astype(bf).reshape(b, nc, tk, WC)
    body = functools.partial(_dsa_prompt_body, tq=tq, tk=tk, att_group=att_group, topk=topk,
                             idx_bits=int(s).bit_length())
    return pl.pallas_call(
        body,
        grid=(b, s // tq),
        in_specs=[
            pl.BlockSpec((1, tq, HI * DI), lambda bi, i: (bi, i, 0)),
            pl.BlockSpec((1, tq, HI), lambda bi, i: (bi, i, 0)),
            pl.BlockSpec((1, tq, WC), lambda bi, i: (bi, i, 0)),
            pl.BlockSpec((1, nc, DI, tk), lambda bi, i: (bi, 0, 0, 0)),
            pl.BlockSpec((1, nc, WC, tk), lambda bi, i: (bi, 0, 0, 0)),
            pl.BlockSpec((1, nc, tk, WC), lambda bi, i: (bi, 0, 0, 0)),
        ],
        out_specs=pl.BlockSpec((1, tq, WC), lambda bi, i: (bi, i, 0)),
        out_shape=jax.ShapeDtypeStruct((b, s, WC), jnp.float32),
        scratch_shapes=[
            pltpu.VMEM((nc, tq, tk), jnp.int32),
            pltpu.VMEM((nc, tk, tq), jnp.int16),
            pltpu.VMEM((nc, tk, tq), jnp.int16),
            pltpu.VMEM((HI, tq, LANES), jnp.float32),
            pltpu.VMEM((tq, LANES), jnp.int32),
            pltpu.VMEM((tq, LANES), jnp.int32),
            pltpu.VMEM((HC, tq, LANES), jnp.float32),
            pltpu.VMEM((HC, tq, LANES), jnp.float32),
            pltpu.VMEM((HC, tq, DHC), jnp.float32),
        ],
        compiler_params=pltpu.CompilerParams(dimension_semantics=("arbitrary", "arbitrary"),
                                             vmem_limit_bytes=VMEM_LIMIT_BYTES),
        name="dsa_prompt",
    )(qi.astype(bf), wi, q.astype(bf), kit, kt, vc)


def _diff_prompt_body(lam_ref, q_ref, kt_ref, v_ref, g_ref, o_ref, m_ref, l_ref, acc_ref,
                      *, tq, tk, out_scale):
    i = pl.program_id(1)
    qpos = i * tq + lax.broadcasted_iota(jnp.int32, (tq, 1), 0)
    m_ref[...] = jnp.full(m_ref.shape, M_INIT, jnp.float32)
    l_ref[...] = jnp.zeros(l_ref.shape, jnp.float32)
    acc_ref[...] = jnp.zeros(acc_ref.shape, jnp.float32)

    def step(c, masked):
        kt = kt_ref[0, c]
        vv = v_ref[0, c]
        if masked:
            kpos = c * tk + lax.broadcasted_iota(jnp.int32, (1, tk), 1)
            bias = jnp.where(kpos <= qpos, 0.0, MASK_BIAS)
        for h in range(HB):
            for comp in range(2):
                o = (h * 2 + comp) * DHB
                s = jnp.dot(q_ref[0, :, o:o + DHB], kt[o:o + DHB, :], preferred_element_type=jnp.float32)
                if masked:
                    s = s + bias
                _flash_update(s, vv[:, h * DVB:(h + 1) * DVB], m_ref, l_ref, acc_ref, h * 2 + comp, DVB)

    def full_body(c, carry):
        step(c, False)
        return carry

    def diag_body(c, carry):
        step(c, True)
        return carry

    n_full = lax.div(i * tq, tk)
    n_end = lax.div((i + 1) * tq + tk - 1, tk)
    lax.fori_loop(0, n_full, full_body, 0)
    lax.fori_loop(n_full, n_end, diag_body, 0)

    lam = lam_ref[0]
    outs = []
    for h in range(HB):
        o1 = acc_ref[2 * h] / l_ref[2 * h]
        o2 = acc_ref[2 * h + 1] / l_ref[2 * h + 1]
        o = o1 - lam * o2
        o = o * lax.rsqrt(jnp.mean(o * o, axis=1, keepdims=True) + DIFF_SUBLN_EPS)
        outs.append(o * g_ref[...] * out_scale)
    o_ref[0] = jnp.concatenate(outs, axis=1)


def diff_prompt_pallas(q, k, v, lam, subln_g, out_scale, *, tq=512, tk=512):
    b, s, _ = q.shape
    tq, tk = min(tq, s), min(tk, s)
    assert s % tq == 0 and s % tk == 0 and (tq % tk == 0 or tk % tq == 0) and DVB == LANES
    nc = s // tk
    bf = jnp.bfloat16
    wq = HB * 2 * DHB
    kt = k.astype(bf).reshape(b, nc, tk, wq).swapaxes(2, 3)
    vc = v.astype(bf).reshape(b, nc, tk, WB)
    body = functools.partial(_diff_prompt_body, tq=tq, tk=tk, out_scale=out_scale)
    return pl.pallas_call(
        body,
        grid=(b, s // tq),
        in_specs=[
            pl.BlockSpec(memory_space=pltpu.SMEM),
            pl.BlockSpec((1, tq, wq), lambda bi, i: (bi, i, 0)),
            pl.BlockSpec((1, nc, wq, tk), lambda bi, i: (bi, 0, 0, 0)),
            pl.BlockSpec((1, nc, tk, WB), lambda bi, i: (bi, 0, 0, 0)),
            pl.BlockSpec((1, DVB), lambda bi, i: (0, 0)),
        ],
        out_specs=pl.BlockSpec((1, tq, WB), lambda bi, i: (bi, i, 0)),
        out_shape=jax.ShapeDtypeStruct((b, s, WB), jnp.float32),
        scratch_shapes=[
            pltpu.VMEM((HB * 2, tq, LANES), jnp.float32),
            pltpu.VMEM((HB * 2, tq, LANES), jnp.float32),
            pltpu.VMEM((HB * 2, tq, DVB), jnp.float32),
        ],
        compiler_params=pltpu.CompilerParams(dimension_semantics=("arbitrary", "arbitrary"),
                                             vmem_limit_bytes=VMEM_LIMIT_BYTES),
        name="diff_prompt",
    )(lam.reshape(1).astype(jnp.float32), q.astype(bf), kt, vc, subln_g.reshape(1, DVB).astype(jnp.float32))


PAGES_PER_STEP = 16
DSA_PAGES_PER_STEP = 32
SELECT_PAGES_PER_STEP = 64


def _order_key(x):
    x = jnp.where(x == 0.0, 0.0, x)
    bits = pltpu.bitcast(x, jnp.int32)
    return bits ^ ((bits >> 31) & 0x7FFFFFFF)


def _page_specs(block, layer, n_rep):
    def spec(g):
        return pl.BlockSpec((None, None) + block,
                            lambda b, j, pt: (layer, pt[b, j * n_rep + g], 0, 0))
    return [spec(g) for g in range(n_rep)]


def _dsa_sample_select_body(pt_ref, qi_ref, w_ref, kin_ref, *rest, pg, nj, t_new, topk, idx_bits):
    page_refs, (bias_ref, key_scr, keyn_scr) = rest[:pg], rest[pg:]
    j = pl.program_id(1)
    width = pg * PAGE_SIZE
    bf = jnp.bfloat16

    def scores(kit):
        qk = jnp.dot(qi_ref[0], kit.astype(bf), preferred_element_type=jnp.float32)
        sc = jnp.maximum(qk, 0.0) * w_ref[0]
        out = sc[0:t_new]
        for h in range(1, HI):
            out = out + sc[h * t_new:(h + 1) * t_new]
        return out

    key_scr[j] = _order_key(scores(jnp.concatenate([r[...] for r in page_refs], axis=1)))

    @pl.when(j == nj - 1)
    def _():
        past = nj * width
        tok = lax.broadcasted_iota(jnp.int32, (t_new, 1), 0)
        lane_n = lax.broadcasted_iota(jnp.int32, (1, PAGE_SIZE), 1)
        sc_n = jnp.where(lane_n <= tok, scores(kin_ref[0]), -jnp.inf)
        keyn_scr[...] = _order_key(sc_n)
        lane_p = lax.broadcasted_iota(jnp.int32, (1, width), 1)
        idx_n = past + lane_n

        def count(pred):
            part = jnp.zeros((t_new, width), jnp.int32)
            for s in range(nj):
                part = part + pred(key_scr[s], s * width + lane_p).astype(jnp.int32)
            return (jnp.sum(part, axis=1, keepdims=True)
                    + jnp.sum(pred(keyn_scr[...], idx_n).astype(jnp.int32), axis=1, keepdims=True))

        def thr_bit(bi, t):
            cand = t + jnp.left_shift(jnp.int32(1), 31 - bi)
            return jnp.where(count(lambda k, idx: k >= cand) >= topk, cand, t)

        thr = lax.fori_loop(0, 32, thr_bit, jnp.full((t_new, 1), INT32_MIN, jnp.int32))
        need = topk - count(lambda k, idx: k > thr)

        def tie_bit(bi, jb):
            cand = jb + jnp.left_shift(jnp.int32(1), idx_bits - 1 - bi)
            before = count(lambda k, idx: (k == thr) & (idx < cand))
            return jnp.where(before <= need - 1, cand, jb)

        tie = lax.fori_loop(0, idx_bits, tie_bit, jnp.zeros((t_new, 1), jnp.int32))

        def bias_of(k, idx):
            return jnp.where((k > thr) | ((k == thr) & (idx <= tie)), 0.0, MASK_BIAS)

        for s in range(nj):
            idx_s = s * width + lax.broadcasted_iota(jnp.int32, (1, width), 1)
            bias_ref[0, :, s * width:(s + 1) * width] = bias_of(key_scr[s], idx_s)
        bias_ref[0, :, past:past + PAGE_SIZE] = jnp.where(lane_n <= tok, bias_of(keyn_scr[...], idx_n), MASK_BIAS)


def _dsa_sample_attn_body(pt_ref, q_ref, bias_ref, biasn_ref, knt_ref, vnt_ref, *rest, pg, t_new):
    k_refs, v_refs = rest[:pg], rest[pg:2 * pg]
    o_ref, m_ref, l_ref, acc_ref = rest[2 * pg:]
    j = pl.program_id(1)
    bf = jnp.bfloat16

    @pl.when(j == 0)
    def _():
        m_ref[...] = jnp.full(m_ref.shape, M_INIT, jnp.float32)
        l_ref[...] = jnp.zeros(l_ref.shape, jnp.float32)
        acc_ref[...] = jnp.zeros(acc_ref.shape, jnp.float32)

    def update(kt, vt, bias):
        s = jnp.dot(q_ref[0], kt.astype(bf), preferred_element_type=jnp.float32)
        s = s + jnp.concatenate([bias] * HC, axis=0)
        m_prev = m_ref[...]
        m_new = jnp.maximum(m_prev, jnp.max(s, axis=1, keepdims=True))
        alpha = jnp.exp(m_prev - m_new)
        p = jnp.exp(s - m_new[:, :1])
        l_ref[...] = alpha * l_ref[...] + jnp.sum(p, axis=1, keepdims=True)
        pv = lax.dot_general(p.astype(bf), vt.astype(bf), (((1,), (1,)), ((), ())),
                             preferred_element_type=jnp.float32)
        acc_ref[...] = acc_ref[...] * jnp.concatenate([alpha] * (WC // LANES), axis=1) + pv
        m_ref[...] = m_new

    update(jnp.concatenate([r[...] for r in k_refs], axis=1),
           jnp.concatenate([r[...] for r in v_refs], axis=1), bias_ref[0])

    @pl.when(j == pl.num_programs(1) - 1)
    def _():
        update(knt_ref[0], vnt_ref[0], biasn_ref[0])
        head_of_lane = lax.broadcasted_iota(jnp.int32, (1, WC), 1) // DHC
        out = jnp.zeros((t_new, WC), jnp.float32)
        for h in range(HC):
            rows = slice(h * t_new, (h + 1) * t_new)
            o_h = acc_ref[rows] / jnp.concatenate([l_ref[rows]] * (WC // LANES), axis=1)
            out = jnp.where(head_of_lane == h, o_h, out)
        o_ref[0] = out


def dsa_sample_pallas(q, k_new, v_new, qi, wi, ki_new, pool_k, pool_v, pool_ki, page_table, layer,
                      *, pg=DSA_PAGES_PER_STEP, pg_select=SELECT_PAGES_PER_STEP):
    db, t_new, _ = q.shape
    n_pages = page_table.shape[1]
    past = n_pages * PAGE_SIZE
    assert n_pages % pg == 0 and n_pages % pg_select == 0 and t_new <= PAGE_SIZE
    nj, nj_s = n_pages // pg, n_pages // pg_select
    width, width_s = pg * PAGE_SIZE, pg_select * PAGE_SIZE
    total = past + PAGE_SIZE
    topk = min(TOPK_MAX, (past + t_new) // 4)
    bf = jnp.bfloat16
    n_layers, n_pool = pool_k.shape[:2]
    kit_pool = pool_ki.transpose(0, 1, 3, 2)
    kt_pool = pool_k.transpose(0, 1, 3, 4, 2).reshape(n_layers, n_pool, WC, PAGE_SIZE)
    vt_pool = pool_v.transpose(0, 1, 3, 4, 2).reshape(n_layers, n_pool, WC, PAGE_SIZE)
    pad_t = lambda x: jnp.pad(x.swapaxes(1, 2), ((0, 0), (0, 0), (0, PAGE_SIZE - t_new)))
    qi2 = qi.reshape(db, t_new, HI, DI).swapaxes(1, 2).reshape(db, HI * t_new, DI).astype(bf)
    w2 = wi.swapaxes(1, 2).reshape(db, HI * t_new, 1)
    seq_spec = lambda shape: pl.BlockSpec((1,) + shape, lambda b, j, pt: (b, 0, 0))
    bias = pl.pallas_call(
        functools.partial(_dsa_sample_select_body, pg=pg_select, nj=nj_s, t_new=t_new, topk=topk,
                          idx_bits=int(total).bit_length()),
        grid_spec=pltpu.PrefetchScalarGridSpec(
            num_scalar_prefetch=1, grid=(db, nj_s),
            in_specs=[seq_spec((HI * t_new, DI)), seq_spec((HI * t_new, 1)), seq_spec((DI, PAGE_SIZE))]
            + _page_specs((DI, PAGE_SIZE), layer, pg_select),
            out_specs=seq_spec((t_new, total)),
            scratch_shapes=[pltpu.VMEM((nj_s, t_new, width_s), jnp.int32),
                            pltpu.VMEM((t_new, PAGE_SIZE), jnp.int32)]),
        out_shape=jax.ShapeDtypeStruct((db, t_new, total), jnp.float32),
        compiler_params=pltpu.CompilerParams(dimension_semantics=("arbitrary", "arbitrary")),
        name="dsa_sample_select",
    )(page_table, qi2, w2, pad_t(ki_new).astype(bf), *([kit_pool] * pg_select))

    qh = q.reshape(db, t_new, HC, DHC).swapaxes(1, 2)
    qbd = (qh[:, :, :, None, :] * jnp.eye(HC, dtype=q.dtype)[None, :, None, :, None])
    qbd = qbd.reshape(db, HC * t_new, WC).astype(bf)
    return pl.pallas_call(
        functools.partial(_dsa_sample_attn_body, pg=pg, t_new=t_new),
        grid_spec=pltpu.PrefetchScalarGridSpec(
            num_scalar_prefetch=1, grid=(db, nj),
            in_specs=[seq_spec((HC * t_new, WC)),
                      pl.BlockSpec((1, t_new, width), lambda b, j, pt: (b, 0, j)),
                      pl.BlockSpec((1, t_new, PAGE_SIZE), lambda b, j, pt: (b, 0, n_pages)),
                      seq_spec((WC, PAGE_SIZE)), seq_spec((WC, PAGE_SIZE))]
            + _page_specs((WC, PAGE_SIZE), layer, pg) + _page_specs((WC, PAGE_SIZE), layer, pg),
            out_specs=seq_spec((t_new, WC)),
            scratch_shapes=[pltpu.VMEM((HC * t_new, LANES), jnp.float32),
                            pltpu.VMEM((HC * t_new, LANES), jnp.float32),
                            pltpu.VMEM((HC * t_new, WC), jnp.float32)]),
        out_shape=jax.ShapeDtypeStruct((db, t_new, WC), jnp.float32),
        compiler_params=pltpu.CompilerParams(dimension_semantics=("arbitrary", "arbitrary")),
        name="dsa_sample_attn",
    )(page_table, qbd, bias, bias, pad_t(k_new).astype(bf), pad_t(v_new).astype(bf),
      *([kt_pool] * pg), *([vt_pool] * pg))


def _diff_sample_body(pt_ref, lam_ref, q_ref, knt_ref, vn_ref, g_ref, *rest, pg, t_new, out_scale):
    k_refs, v_refs = rest[:pg], rest[pg:2 * pg]
    o_ref, m_ref, l_ref, acc_ref = rest[2 * pg:]
    j = pl.program_id(1)
    bf = jnp.bfloat16

    @pl.when(j == 0)
    def _():
        m_ref[...] = jnp.full(m_ref.shape, M_INIT, jnp.float32)
        l_ref[...] = jnp.zeros(l_ref.shape, jnp.float32)
        acc_ref[...] = jnp.zeros(acc_ref.shape, jnp.float32)

    def update(kt, v, bias):
        s = jnp.dot(q_ref[0], kt.astype(bf), preferred_element_type=jnp.float32)
        if bias is not None:
            s = s + bias
        m_prev = m_ref[...]
        m_new = jnp.maximum(m_prev, jnp.max(s, axis=1, keepdims=True))
        alpha = jnp.exp(m_prev - m_new)
        p = jnp.exp(s - m_new[:, :1])
        l_ref[...] = alpha * l_ref[...] + jnp.sum(p, axis=1, keepdims=True)
        pv = jnp.dot(p.astype(bf), v.astype(bf), preferred_element_type=jnp.float32)
        acc_ref[...] = acc_ref[...] * jnp.concatenate([alpha] * HB, axis=1) + pv
        m_ref[...] = m_new

    v_pages = [jnp.concatenate([r[pl.ds(h, PAGE_SIZE, stride=HB), :] for h in range(HB)], axis=1) for r in v_refs]
    update(jnp.concatenate([r[...] for r in k_refs], axis=1), jnp.concatenate(v_pages, axis=0), None)

    @pl.when(j == pl.num_programs(1) - 1)
    def _():
        n_rows = 2 * HB * t_new
        tok = lax.broadcasted_iota(jnp.int32, (n_rows, 1), 0) % t_new
        lane_n = lax.broadcasted_iota(jnp.int32, (1, PAGE_SIZE), 1)
        update(knt_ref[0], vn_ref[0], jnp.where(lane_n <= tok, 0.0, MASK_BIAS))
        lam = lam_ref[0]
        outs = []
        for h in range(HB):
            r1 = slice(2 * h * t_new, (2 * h + 1) * t_new)
            r2 = slice((2 * h + 1) * t_new, (2 * h + 2) * t_new)
            cols = slice(h * DVB, (h + 1) * DVB)
            o = acc_ref[r1, cols] / l_ref[r1] - lam * (acc_ref[r2, cols] / l_ref[r2])
            o = o * lax.rsqrt(jnp.mean(o * o, axis=1, keepdims=True) + DIFF_SUBLN_EPS)
            outs.append(o * g_ref[...] * out_scale)
        o_ref[0] = jnp.concatenate(outs, axis=1)


def diff_sample_pallas(q, k_new, v_new, pool_k, pool_v, page_table, layer, lam, subln_g, out_scale,
                       *, pg=PAGES_PER_STEP):
    db, t_new, wq = q.shape
    n_pages = page_table.shape[1]
    assert n_pages % pg == 0 and t_new <= PAGE_SIZE and DVB == LANES
    nj = n_pages // pg
    bf = jnp.bfloat16
    n_layers, n_pool = pool_k.shape[:2]
    kt_pool = pool_k.transpose(0, 1, 3, 4, 5, 2).reshape(n_layers, n_pool, wq, PAGE_SIZE)
    v_pool = pool_v.reshape(n_layers, n_pool, PAGE_SIZE * HB, DVB)
    n_hc = 2 * HB
    qh = q.reshape(db, t_new, n_hc, DHB).swapaxes(1, 2)
    qbd = (qh[:, :, :, None, :] * jnp.eye(n_hc, dtype=q.dtype)[None, :, None, :, None])
    qbd = qbd.reshape(db, n_hc * t_new, wq).astype(bf)
    knt = jnp.pad(k_new.swapaxes(1, 2), ((0, 0), (0, 0), (0, PAGE_SIZE - t_new))).astype(bf)
    vn = jnp.pad(v_new, ((0, 0), (0, PAGE_SIZE - t_new), (0, 0))).astype(bf)
    seq_spec = lambda shape: pl.BlockSpec((1,) + shape, lambda b, j, pt: (b, 0, 0))
    return pl.pallas_call(
        functools.partial(_diff_sample_body, pg=pg, t_new=t_new, out_scale=out_scale),
        grid_spec=pltpu.PrefetchScalarGridSpec(
            num_scalar_prefetch=1, grid=(db, nj),
            in_specs=[pl.BlockSpec(memory_space=pltpu.SMEM),
                      seq_spec((n_hc * t_new, wq)), seq_spec((wq, PAGE_SIZE)), seq_spec((PAGE_SIZE, WB)),
                      pl.BlockSpec((1, DVB), lambda b, j, pt: (0, 0))]
            + _page_specs((wq, PAGE_SIZE), layer, pg) + _page_specs((PAGE_SIZE * HB, DVB), layer, pg),
            out_specs=seq_spec((t_new, WB)),
            scratch_shapes=[pltpu.VMEM((n_hc * t_new, LANES), jnp.float32),
                            pltpu.VMEM((n_hc * t_new, LANES), jnp.float32),
                            pltpu.VMEM((n_hc * t_new, WB), jnp.float32)]),
        out_shape=jax.ShapeDtypeStruct((db, t_new, WB), jnp.float32),
        compiler_params=pltpu.CompilerParams(dimension_semantics=("arbitrary", "arbitrary"),
                                             vmem_limit_bytes=VMEM_LIMIT_BYTES),
        name="diff_sample",
    )(page_table, lam.reshape(1).astype(jnp.float32), qbd, knt, vn,
      subln_g.reshape(1, DVB).astype(jnp.float32), *([kt_pool] * pg), *([v_pool] * pg))


RWKV_TBLK = LANES // 2
RWKV_PAIRS = HA // 2


def _rwkv_scan_body(r_ref, w_ref, k_ref, kk_ref, b_ref, vt_ref, e_ref, s0_ref, ot_ref, sout_ref,
                    s_scr, vb_scr, sr_scr, *, bb, nsteps):
    ti = pl.program_id(1)

    @pl.when(ti == 0)
    def _():
        s_scr[...] = s0_ref[...]

    slot = lax.broadcasted_iota(jnp.int32, (1, LANES), 1) & (RWKV_TBLK - 1)
    e = e_ref[...]
    bf = jnp.bfloat16
    units = [(bi, p) for bi in range(bb) for p in range(RWKV_PAIRS)]

    for u, (bi, p) in enumerate(units):
        vt = vt_ref[bi, p, 0]
        lhs = jnp.concatenate([jnp.where(slot == t, vt, 0.0).astype(bf) for t in range(nsteps)], axis=0)
        vb_scr[u] = jnp.dot(lhs, e, preferred_element_type=jnp.float32).reshape(nsteps, NA, LANES)

    def step(t, carry):
        rows = [[ref[bi, pl.ds(t, 1), :] for ref in (r_ref, w_ref, k_ref, kk_ref, b_ref)] for bi in range(bb)]
        ms = [(s_scr[bi, p] * rows[bi][3][:, p * LANES:(p + 1) * LANES]).astype(bf) for bi, p in units]
        sk_all = jnp.dot(jnp.concatenate(ms, axis=0), e, preferred_element_type=jnp.float32)
        for u, (bi, p) in enumerate(units):
            cols = slice(p * LANES, (p + 1) * LANES)
            r_t, w_t, k_t, _, b_t = rows[bi]
            sk = sk_all[u * NA:(u + 1) * NA]
            s = s_scr[bi, p] * w_t[:, cols] - sk * b_t[:, cols] + vb_scr[u, t] * k_t[:, cols]
            s_scr[bi, p] = s
            sr_scr[u, t] = s * r_t[:, cols]
        return carry

    lax.fori_loop(0, nsteps, step, 0)

    for u, (bi, p) in enumerate(units):
        o_b = jnp.dot(sr_scr[u].reshape(nsteps * NA, LANES).astype(bf), e, preferred_element_type=jnp.float32)
        ot = jnp.zeros((NA, LANES), jnp.float32)
        for t in range(nsteps):
            ot = jnp.where(slot == t, o_b[t * NA:(t + 1) * NA], ot)
        ot_ref[bi, p, 0] = ot

    @pl.when(ti == pl.num_programs(1) - 1)
    def _():
        sout_ref[...] = s_scr[...]


def rwkv_scan_pallas(r, w, k, v, kk, b, s0, *, bb=2):
    nb, t, _ = r.shape
    nsteps = min(RWKV_TBLK, t)
    assert nb % bb == 0 and t % nsteps == 0 and LANES == 2 * NA
    nblk = t // nsteps
    vt = v.reshape(nb, nblk, nsteps, RWKV_PAIRS, 2, NA).transpose(0, 3, 1, 5, 4, 2)
    vt = jnp.pad(vt, ((0, 0),) * 5 + ((0, RWKV_TBLK - nsteps),)).reshape(nb, RWKV_PAIRS, nblk, NA, LANES)
    s0p = s0.reshape(nb, RWKV_PAIRS, 2, NA, NA).transpose(0, 1, 3, 2, 4).reshape(nb, RWKV_PAIRS, NA, LANES)
    head_of_lane = jnp.arange(LANES) // NA
    e = (head_of_lane[:, None] == head_of_lane[None, :]).astype(jnp.bfloat16)
    row_spec = pl.BlockSpec((bb, nsteps, WA), lambda bi, ti: (bi, ti, 0))
    col_spec = pl.BlockSpec((bb, RWKV_PAIRS, 1, NA, LANES), lambda bi, ti: (bi, 0, ti, 0, 0))
    st_spec = pl.BlockSpec((bb, RWKV_PAIRS, NA, LANES), lambda bi, ti: (bi, 0, 0, 0))
    ot, s_fin = pl.pallas_call(
        functools.partial(_rwkv_scan_body, bb=bb, nsteps=nsteps),
        grid=(nb // bb, nblk),
        in_specs=[row_spec, row_spec, row_spec, row_spec, row_spec, col_spec,
                  pl.BlockSpec((LANES, LANES), lambda bi, ti: (0, 0)), st_spec],
        out_specs=[col_spec, st_spec],
        out_shape=[jax.ShapeDtypeStruct((nb, RWKV_PAIRS, nblk, NA, LANES), jnp.float32),
                   jax.ShapeDtypeStruct((nb, RWKV_PAIRS, NA, LANES), jnp.float32)],
        scratch_shapes=[pltpu.VMEM((bb, RWKV_PAIRS, NA, LANES), jnp.float32),
                        pltpu.VMEM((bb * RWKV_PAIRS, nsteps, NA, LANES), jnp.float32),
                        pltpu.VMEM((bb * RWKV_PAIRS, nsteps, NA, LANES), jnp.float32)],
        compiler_params=pltpu.CompilerParams(dimension_semantics=("arbitrary", "arbitrary"),
                                             vmem_limit_bytes=VMEM_LIMIT_BYTES),
        name="rwkv_scan",
    )(r, w, k, kk, b, vt, e, s0p)
    o = ot.reshape(nb, RWKV_PAIRS, nblk, NA, 2, RWKV_TBLK)[..., :nsteps]
    o = o.transpose(0, 2, 5, 1, 4, 3).reshape(nb, t, WA)
    s_fin = s_fin.reshape(nb, RWKV_PAIRS, NA, 2, NA).transpose(0, 1, 3, 2, 4).reshape(nb, HA, NA, NA)
    return o, s_fin


ROW_TILE = 256
N_MAIN = NCA + NCB + NCC
N_MAIN_PAD = -(-N_MAIN // LANES) * LANES
PROJ_SEGMENTS = (
    ("pa", 0, NCA, jnp.float32, 1.0),
    ("qb", NCA, HB * 2 * DHB, jnp.bfloat16, DHB ** -0.5),
    ("kb", NCA + HB * 2 * DHB, HB * 2 * DHB, jnp.float32, 1.0),
    ("vb", NCA + 2 * HB * 2 * DHB, WB, jnp.float32, 1.0),
    ("qc", NCA + NCB, WC, jnp.bfloat16, DHC ** -0.5),
    ("kc", NCA + NCB + WC, WC, jnp.float32, 1.0),
    ("vc", NCA + NCB + 2 * WC, WC, jnp.float32, 1.0),
    ("qi", NCA + NCB + 3 * WC, HI * DI, jnp.bfloat16, DI ** -0.5),
)
KI_START = NCA + NCB + 3 * WC + HI * DI


def _proj_body(x_ref, g_ref, wm_ref, wg_ref, *out_refs):
    seg_refs, (ki_ref, wi_ref, gate_ref) = out_refs[:len(PROJ_SEGMENTS)], out_refs[len(PROJ_SEGMENTS):]
    x = x_ref[...]
    xn = (x * lax.rsqrt(jnp.mean(x * x, axis=-1, keepdims=True) + NORM_EPS) * g_ref[...]).astype(jnp.bfloat16)
    for (name, start, width, dtype, scale), ref in zip(PROJ_SEGMENTS, seg_refs):
        y = jnp.dot(xn, wm_ref[:, start:start + width], preferred_element_type=jnp.float32)
        ref[...] = (y * scale if scale != 1.0 else y).astype(dtype)
    tail = jnp.dot(xn, wm_ref[:, KI_START:KI_START + LANES], preferred_element_type=jnp.float32)
    ki_ref[...] = tail[:, :DI]
    wi_ref[...] = tail[:, DI:DI + HI] * (HI ** -0.5)
    for c in range(N_BRANCH):
        y = jnp.dot(xn, wg_ref[:, c * D_MODEL:(c + 1) * D_MODEL], preferred_element_type=jnp.float32)
        gate_ref[:, c * D_MODEL:(c + 1) * D_MODEL] = jax.nn.sigmoid(y)


def proj_pallas(x2d, g, w_in_l):
    t, d = x2d.shape
    tm = min(ROW_TILE, t)
    assert t % tm == 0 and KI_START % LANES == 0 and KI_START + DI + HI == N_MAIN
    bf = jnp.bfloat16
    w_main = jnp.pad(w_in_l[:, :N_MAIN], ((0, 0), (0, N_MAIN_PAD - N_MAIN))).astype(bf)
    w_gate = w_in_l[:, N_MAIN:].astype(bf)
    row = lambda width: pl.BlockSpec((tm, width), lambda i: (i, 0))
    full = lambda a: pl.BlockSpec(a.shape, lambda i: (0, 0))
    widths = [s[2] for s in PROJ_SEGMENTS] + [DI, HI, N_BRANCH * D_MODEL]
    dtypes = [s[3] for s in PROJ_SEGMENTS] + [jnp.float32] * 3
    outs = pl.pallas_call(
        _proj_body,
        grid=(t // tm,),
        in_specs=[row(d), pl.BlockSpec((1, d), lambda i: (0, 0)), full(w_main), full(w_gate)],
        out_specs=[row(w) for w in widths],
        out_shape=[jax.ShapeDtypeStruct((t, w), dt) for w, dt in zip(widths, dtypes)],
        compiler_params=pltpu.CompilerParams(dimension_semantics=("arbitrary",),
                                             vmem_limit_bytes=VMEM_LIMIT_BYTES),
        name="in_proj",
    )(x2d, g.reshape(1, d), w_main, w_gate)
    names = [s[0] for s in PROJ_SEGMENTS] + ["ki", "wi", "gates"]
    return dict(zip(names, outs))


def _head_sum(x, e):
    x_hi = x.astype(jnp.bfloat16)
    x_lo = (x - x_hi.astype(jnp.float32)).astype(jnp.bfloat16)
    return (jnp.dot(x_hi, e, preferred_element_type=jnp.float32)
            + jnp.dot(x_lo, e, preferred_element_type=jnp.float32))


def _head_ones():
    head_of_lane = jnp.arange(WA) // NA
    return (head_of_lane[:, None] == head_of_lane[None, :]).astype(jnp.bfloat16)


def _rwkv_pre_body(p_ref, ps_ref, mu_ref, w0_ref, wup_ref, a0_ref, aup_ref, gup_ref, kk_ref, ka_ref, rk_ref,
                   e_ref, r_o, w_o, k_o, v_o, kk_o, b_o, g_o, bonus_o):
    bf = jnp.bfloat16
    e = e_ref[...]
    p = p_ref[...]
    px = p + (ps_ref[...] - p) * mu_ref[...]
    r, k, v = px[:, 0:WA], px[:, WA:2 * WA], px[:, 2 * WA:3 * WA]
    o = 3 * WA
    wd, ad, gd = px[:, o:o + LORA_W], px[:, o + LORA_W:o + LORA_W + LORA_A], px[:, o + LORA_W + LORA_A:]
    dot = lambda x, w_ref: jnp.dot(x.astype(bf), w_ref[...], preferred_element_type=jnp.float32)
    z = -(w0_ref[...] + dot(jnp.tanh(wd), wup_ref))
    softplus = jnp.maximum(z, 0.0) + jnp.log(1.0 + jnp.exp(-jnp.abs(z)))
    decay = jnp.exp(-jnp.exp(-softplus - 0.5))
    a = jax.nn.sigmoid(a0_ref[...] + dot(ad, aup_ref))
    kk = k * kk_ref[...]
    kk = kk * lax.rsqrt(jnp.maximum(_head_sum(kk * kk, e), 1e-24))
    k = k * (1.0 + (a - 1.0) * ka_ref[...])
    r_o[...] = r
    w_o[...] = decay
    k_o[...] = k
    v_o[...] = v
    kk_o[...] = kk
    b_o[...] = kk * a
    g_o[...] = dot(jax.nn.sigmoid(gd), gup_ref)
    bonus_o[...] = _head_sum(r * k * rk_ref[...], e) * v


def rwkv_pre_pallas(pa2d, ps2d, mu, w0, w_up, a0, a_up, g_up, k_k, k_a, r_k):
    t, _ = pa2d.shape
    tm = min(ROW_TILE, t)
    bf = jnp.bfloat16
    vecs = [x.reshape(1, -1).astype(jnp.float32) for x in (mu, w0, a0, k_k, k_a, r_k)]
    mats = [x.astype(bf) for x in (w_up, a_up, g_up)]
    row = lambda width: pl.BlockSpec((tm, width), lambda i: (i, 0))
    full = lambda a: pl.BlockSpec(a.shape, lambda i: (0, 0))
    args = [pa2d, ps2d, vecs[0], vecs[1], mats[0], vecs[2], mats[1], mats[2], vecs[3], vecs[4], vecs[5], _head_ones()]
    return pl.pallas_call(
        _rwkv_pre_body,
        grid=(t // tm,),
        in_specs=[row(NCA), row(NCA)] + [full(a) for a in args[2:]],
        out_specs=[row(WA)] * 8,
        out_shape=[jax.ShapeDtypeStruct((t, WA), jnp.float32)] * 8,
        compiler_params=pltpu.CompilerParams(dimension_semantics=("arbitrary",)),
        name="rwkv_pre",
    )(*args)


def _merge_body(h_ref, o_ref, bonus_ref, g_ref, lnw_ref, lnb_ref, e_ref, ob_ref, oc_ref, gate_ref,
                wa_ref, wb_ref, wc_ref, wo_ref, out_ref):
    bf = jnp.bfloat16
    e = e_ref[...]
    o = o_ref[...]
    c = o - _head_sum(o, e) * (1.0 / NA)
    var = _head_sum(c * c, e) * (1.0 / NA)
    oa = (c * lax.rsqrt(var + RWKV_GN_EPS) * lnw_ref[...] + lnb_ref[...] + bonus_ref[...]) * g_ref[...]
    merged = None
    for ci, (x, w_ref) in enumerate(((oa, wa_ref), (ob_ref[...], wb_ref), (oc_ref[...], wc_ref))):
        br = jnp.dot(x.astype(bf), w_ref[...], preferred_element_type=jnp.float32)
        term = gate_ref[:, ci * D_MODEL:(ci + 1) * D_MODEL] * br
        merged = term if merged is None else merged + term
    out_ref[...] = h_ref[...] + jnp.dot(merged.astype(bf), wo_ref[...], preferred_element_type=jnp.float32)


def merge_pallas(h2d, o_rwkv, bonus, g_rwkv, ln_w, ln_b, ob, oc, gates, w_a, w_b, w_c, w_o):
    t, d = h2d.shape
    tm = min(ROW_TILE, t)
    bf = jnp.bfloat16
    ws = [w.astype(bf) for w in (w_a, w_b, w_c, w_o)]
    row = lambda a: pl.BlockSpec((tm, a.shape[1]), lambda i: (i, 0))
    full = lambda a: pl.BlockSpec(a.shape, lambda i: (0, 0))
    lnw, lnb, e = ln_w.reshape(1, WA), ln_b.reshape(1, WA), _head_ones()
    return pl.pallas_call(
        _merge_body,
        grid=(t // tm,),
        in_specs=[row(a) for a in (h2d, o_rwkv, bonus, g_rwkv)] + [full(a) for a in (lnw, lnb, e)]
        + [row(a) for a in (ob, oc, gates)] + [full(w) for w in ws],
        out_specs=row(h2d),
        out_shape=jax.ShapeDtypeStruct((t, d), jnp.float32),
        compiler_params=pltpu.CompilerParams(dimension_semantics=("arbitrary",),
                                             vmem_limit_bytes=VMEM_LIMIT_BYTES),
        name="branch_merge",
    )(h2d, o_rwkv, bonus, g_rwkv, lnw, lnb, e, ob, oc, gates, *ws)


FFN_CHUNK = 256


def _ffn_body(h_ref, g_ref, wg_ref, wu_ref, wd_ref, gf_ref, o_ref, acc_ref, *, final_norm):
    bf = jnp.bfloat16
    h = h_ref[...]
    xn = (h * lax.rsqrt(jnp.mean(h * h, axis=-1, keepdims=True) + NORM_EPS) * g_ref[...]).astype(bf)
    acc_ref[...] = h
    for f in range(0, D_FF, FFN_CHUNK):
        gate = jnp.dot(xn, wg_ref[:, f:f + FFN_CHUNK], preferred_element_type=jnp.float32)
        up = jnp.dot(xn, wu_ref[:, f:f + FFN_CHUNK], preferred_element_type=jnp.float32)
        act = (gate * jax.nn.sigmoid(gate) * up).astype(bf)
        acc_ref[...] += jnp.dot(act, wd_ref[f:f + FFN_CHUNK, :], preferred_element_type=jnp.float32)
    y = acc_ref[...]
    if final_norm:
        y = y * lax.rsqrt(jnp.mean(y * y, axis=-1, keepdims=True) + NORM_EPS) * gf_ref[...]
    o_ref[...] = y


def ffn_pallas(h2d, g, w_gate, w_up, w_down, g_final, *, final_norm):
    t, d = h2d.shape
    tm = min(ROW_TILE, t)
    assert D_FF % FFN_CHUNK == 0
    bf = jnp.bfloat16
    ws = [w.astype(bf) for w in (w_gate, w_up, w_down)]
    row = pl.BlockSpec((tm, d), lambda i: (i, 0))
    vec = pl.BlockSpec((1, d), lambda i: (0, 0))
    full = lambda a: pl.BlockSpec(a.shape, lambda i: (0, 0))
    return pl.pallas_call(
        functools.partial(_ffn_body, final_norm=final_norm),
        grid=(t // tm,),
        in_specs=[row, vec] + [full(w) for w in ws] + [vec],
        out_specs=row,
        out_shape=jax.ShapeDtypeStruct((t, d), jnp.float32),
        scratch_shapes=[pltpu.VMEM((tm, d), jnp.float32)],
        compiler_params=pltpu.CompilerParams(dimension_semantics=("arbitrary",),
                                             vmem_limit_bytes=VMEM_LIMIT_BYTES),
        name="ffn",
    )(h2d, g.reshape(1, d), *ws, g_final.reshape(1, d))


def _layer_group(h, shift_prev, state0, layer, attend, p):
    b, t, d = h.shape
    n = b * t
    pr = proj_pallas(h.reshape(n, d), p["g_mix"][layer], p["w_in"][layer])
    pa = pr["pa"].reshape(b, t, NCA)
    pa_shift = jnp.concatenate([shift_prev[:, None], pa[:, :-1]], axis=1)
    r, decay, k, v, kk, kk_a, g_rwkv, bonus = rwkv_pre_pallas(
        pr["pa"], pa_shift.reshape(n, NCA), p["rwkv_mu"][layer], p["rwkv_w0"][layer], p["rwkv_w_up"][layer],
        p["rwkv_a0"][layer], p["rwkv_a_up"][layer], p["rwkv_g_up"][layer], p["rwkv_k_k"][layer],
        p["rwkv_k_a"][layer], p["rwkv_r_k"][layer])
    seq = lambda x: x.reshape(b, t, WA)
    o_rwkv, s_fin = rwkv_scan_pallas(seq(r), seq(decay), seq(k), seq(v), seq(kk), seq(kk_a), state0)
    ob, oc = attend(pr)
    h2 = merge_pallas(h.reshape(n, d), o_rwkv.reshape(n, WA), bonus, g_rwkv, p["rwkv_ln_w"][layer],
                      p["rwkv_ln_b"][layer], ob, oc, pr["gates"], p["w_br_a"][layer], p["w_br_b"][layer],
                      p["w_br_c"][layer], p["w_out"][layer])
    h3 = ffn_pallas(h2, p["g_ffn"][layer], p["w_ffn_gate"][layer], p["w_ffn_up"][layer], p["w_ffn_down"][layer],
                    p["g_final"], final_norm=(layer == DEPTH - 1))
    caches = (s_fin, pa[:, -1], pr["kb"].reshape(b, t, HB, 2, DHB), pr["vb"].reshape(b, t, HB, DVB),
              pr["kc"].reshape(b, t, HC, DHC), pr["vc"].reshape(b, t, HC, DHC), pr["ki"].reshape(b, t, DI))
    return h3.reshape(b, t, d), caches


def kernel(x_prompt, x_sample, state_rwkv, state_shift, cache_diff_k, cache_diff_v, cache_dsa_k,
           cache_dsa_v, cache_idx_k, page_table, g_mix, w_in, rwkv_mu, rwkv_w0, rwkv_w_up, rwkv_a0,
           rwkv_a_up, rwkv_g_up, rwkv_k_k, rwkv_k_a, rwkv_r_k, rwkv_ln_w, rwkv_ln_b, diff_lam_q1,
           diff_lam_k1, diff_lam_q2, diff_lam_k2, diff_subln_g, w_br_a, w_br_b, w_br_c, w_out, g_ffn,
           w_ffn_gate, w_ffn_up, w_ffn_down, g_final):
    p = dict(g_mix=g_mix, w_in=w_in, rwkv_mu=rwkv_mu, rwkv_w0=rwkv_w0, rwkv_w_up=rwkv_w_up, rwkv_a0=rwkv_a0,
             rwkv_a_up=rwkv_a_up, rwkv_g_up=rwkv_g_up, rwkv_k_k=rwkv_k_k, rwkv_k_a=rwkv_k_a, rwkv_r_k=rwkv_r_k,
             rwkv_ln_w=rwkv_ln_w, rwkv_ln_b=rwkv_ln_b, w_br_a=w_br_a, w_br_b=w_br_b, w_br_c=w_br_c, w_out=w_out,
             g_ffn=g_ffn, w_ffn_gate=w_ffn_gate, w_ffn_up=w_ffn_up, w_ffn_down=w_ffn_down, g_final=g_final)
    bp, tp = x_prompt.shape[:2]
    bs, ts = x_sample.shape[:2]
    newp, news = [], []
    hp, hs = x_prompt, x_sample
    for l in range(DEPTH):
        lam_init = 0.8 - 0.6 * math.exp(-0.3 * l)
        lam = (jnp.exp(jnp.sum(diff_lam_q1[l] * diff_lam_k1[l])) - jnp.exp(jnp.sum(diff_lam_q2[l] * diff_lam_k2[l]))
               + lam_init).astype(jnp.float32)

        def attend_prompt(pr):
            seq = lambda x: x.reshape(bp, tp, -1)
            ob = diff_prompt_pallas(seq(pr["qb"]), seq(pr["kb"]), seq(pr["vb"]), lam, diff_subln_g[l], 1.0 - lam_init)
            oc = dsa_prompt_pallas(seq(pr["qc"]), seq(pr["kc"]), seq(pr["vc"]), seq(pr["qi"]), seq(pr["wi"]),
                                   seq(pr["ki"]))
            return ob.reshape(bp * tp, WB), oc.reshape(bp * tp, WC)

        def attend_sample(pr):
            seq = lambda x: x.reshape(bs, ts, -1)
            ob = diff_sample_pallas(seq(pr["qb"]), seq(pr["kb"]), seq(pr["vb"]), cache_diff_k, cache_diff_v,
                                    page_table, l, lam, diff_subln_g[l], 1.0 - lam_init)
            oc = dsa_sample_pallas(seq(pr["qc"]), seq(pr["kc"]), seq(pr["vc"]), seq(pr["qi"]), seq(pr["wi"]),
                                   seq(pr["ki"]), cache_dsa_k, cache_dsa_v, cache_idx_k, page_table, l)
            return ob.reshape(bs * ts, WB), oc.reshape(bs * ts, WC)

        hp, cp = _layer_group(hp, jnp.zeros((bp, NCA), jnp.float32), jnp.zeros((bp, HA, NA, NA), jnp.float32),
                              l, attend_prompt, p)
        newp.append(cp)
        hs, cs = _layer_group(hs, state_shift[l], state_rwkv[l], l, attend_sample, p)
        news.append(cs)

    stack = lambda group: tuple(jnp.stack([layer[i] for layer in group]) for i in range(7))
    return (hp, hs) + stack(newp) + stack(news)
```
